```python
import jax, jax.numpy as jnp
from jax import lax
import numpy as np

D_MODEL = 1024
BATCH = 4
SEQ = 8192
DEPTH = 4

GRID_W = 64
CTX_LEN = 256
N_MIXERS = 2
HGRN_HEAD_DIM = 128
HGRN_HEADS = D_MODEL // HGRN_HEAD_DIM
CHUNK = 64
CONV_WIDTH = 3
D_FF = 4 * D_MODEL
N_REC_LAYERS = (DEPTH + N_MIXERS - 1) // N_MIXERS
N_CONV_LAYERS = DEPTH // N_MIXERS
EPS = 1e-6

kernel_name = 'hybrid_hgrn2_shortconv_dit'


def _rmsnorm(x, gain):
    x32 = x.astype(jnp.float32)
    y = x32 * lax.rsqrt(jnp.mean(x32 * x32, axis=-1, keepdims=True) + EPS)
    return (y * gain.astype(jnp.float32)).astype(x.dtype)


def _modulate(h, shift, scale):
    return h * (1 + scale) + shift


def _mlp(h, w1, w2):
    return jnp.square(jax.nn.relu(h @ w1)) @ w2


def _heads(t):
    return t.reshape(*t.shape[:-1], HGRN_HEADS, HGRN_HEAD_DIM).astype(jnp.float32)


def _to_chunks(t):
    b, l, h, e = t.shape
    return t.reshape(b, l // CHUNK, CHUNK, h, e).transpose(1, 0, 3, 2, 4)


def _gla_chunkwise(q, k, v, g, s0):
    bsz, l, h, _ = q.shape
    lower = jnp.tril(jnp.ones((CHUNK, CHUNK), dtype=bool))

    def step(s, blk):
        qb, kb, vb, gb = blk
        cum = jnp.cumsum(gb, axis=2)
        ref = cum[:, :, CHUNK // 2 - 1:CHUNK // 2]
        last = cum[:, :, -1:]
        o_inter = jnp.einsum('bhck,bhkv->bhcv', qb * jnp.exp(cum), s)
        scores = jnp.einsum('bhck,bhsk->bhcs', qb * jnp.exp(cum - ref), kb * jnp.exp(ref - cum))
        scores = jnp.where(lower, scores, 0.0)
        o_intra = jnp.einsum('bhcs,bhsv->bhcv', scores, vb)
        s_new = jnp.exp(last[:, :, 0, :, None]) * s + jnp.einsum(
            'bhsk,bhsv->bhkv', kb * jnp.exp(last - cum), vb)
        return s_new, o_inter + o_intra

    s_fin, o = lax.scan(step, s0, (_to_chunks(q), _to_chunks(k), _to_chunks(v), _to_chunks(g)))
    o = o.transpose(1, 0, 3, 2, 4).reshape(bsz, l, h, v.shape[-1])
    return o, s_fin


def _gla_final_state(k, v, g):
    cum = jnp.cumsum(g, axis=1)
    return jnp.einsum('blhk,blhv->bhkv', k * jnp.exp(cum[:, -1:] - cum), v)


def _flip(t, direction):
    return jnp.flip(t, axis=1) if direction == 1 else t


def _hgrn2_inputs(h, w_in, lb, with_query):
    n_parts = 5 if with_query else 3
    parts = jnp.split(h @ w_in[:, :n_parts * D_MODEL], n_parts, axis=-1)
    v = _heads(parts[2])
    dirs = []
    for d in range(2):
        lb_d = lb[d].reshape(HGRN_HEADS, HGRN_HEAD_DIM)
        f = lb_d + (1.0 - lb_d) * jax.nn.sigmoid(_heads(parts[d]))
        dirs.append((1.0 - f, jnp.log(f)))
    if with_query:
        return v, dirs, jax.nn.silu(_heads(parts[3])), parts[4]
    return v, dirs, None, None


def _hgrn2_readout(o, gate, gnorm, w_out, dtype):
    o = o * lax.rsqrt(jnp.mean(o * o, axis=-1, keepdims=True) + EPS)
    o = o * gnorm.astype(jnp.float32).reshape(HGRN_HEADS, HGRN_HEAD_DIM) * jax.nn.silu(_heads(gate))
    return o.reshape(*o.shape[:-2], D_MODEL).astype(dtype) @ w_out


def _hgrn2_mixer(h, hc, w_in, lb, gnorm, w_out, ctx_out):
    v, dirs, q, gate = _hgrn2_inputs(h, w_in, lb, True)
    vc, dirs_c, qc, gate_c = _hgrn2_inputs(hc, w_in, lb, ctx_out)
    bsz = h.shape[0]
    o_dirs, oc_dirs = [], []
    for d in range(2):
        k, g = dirs[d]
        kc, gc = dirs_c[d]
        if ctx_out:
            s0 = jnp.zeros((bsz, HGRN_HEADS, HGRN_HEAD_DIM, HGRN_HEAD_DIM), jnp.float32)
            oc_d, s_ctx = _gla_chunkwise(_flip(qc, d), _flip(kc, d), _flip(vc, d), _flip(gc, d), s0)
            oc_dirs.append(_flip(oc_d, d))
        else:
            s_ctx = _gla_final_state(_flip(kc, d), _flip(vc, d), _flip(gc, d))
        o_d, _ = _gla_chunkwise(_flip(q, d), _flip(k, d), _flip(v, d), _flip(g, d), s_ctx)
        o_dirs.append(_flip(o_d, d))
    y = _hgrn2_readout(o_dirs[0] + o_dirs[1], gate, gnorm, w_out, h.dtype)
    yc = _hgrn2_readout(oc_dirs[0] + oc_dirs[1], gate_c, gnorm, w_out, hc.dtype) if ctx_out else None
    return y, yc


def _dwconv(z, w, b, axis):
    n = z.shape[axis]
    half = CONV_WIDTH // 2
    pad = [(0, 0)] * z.ndim
    pad[axis] = (half, half)
    zp = jnp.pad(z, pad)
    y = b
    for tap in range(CONV_WIDTH):
        y = y + w[tap] * lax.slice_in_dim(zp, tap, tap + n, axis=axis)
    return y


def _shortconv_mixer(h, w_in, w, b, w_out, axis):
    gate_b, gate_c, xin = jnp.split(h @ w_in, 3, axis=-1)
    return (gate_b * _dwconv(gate_c * xin, w, b, axis)) @ w_out


def setup_inputs(seed: int = 0) -> dict:
    key = jax.random.key(seed)
    ks = jax.random.split(key, 19)
    nrm = jax.random.normal
    f32 = jnp.float32
    d = D_MODEL
    return {
        'x': nrm(ks[0], (BATCH, SEQ, d), f32),
        'c': nrm(ks[1], (BATCH, d), f32),
        'ctx': nrm(ks[2], (BATCH, CTX_LEN, d), f32),
        'c_ctx': nrm(ks[3], (d,), f32),
        'ada_w': nrm(ks[4], (DEPTH, d, 6 * d), f32) * (0.5 * d ** -0.5),
        'ada_b': 0.02 * nrm(ks[5], (DEPTH, 6 * d), f32),
        'norm1': 1.0 + 0.02 * nrm(ks[6], (DEPTH, d), f32),
        'norm2': 1.0 + 0.02 * nrm(ks[7], (DEPTH, d), f32),
        'norm_f': 1.0 + 0.02 * nrm(ks[8], (d,), f32),
        'mlp_w1': nrm(ks[9], (DEPTH, d, D_FF), f32) * d ** -0.5,
        'mlp_w2': nrm(ks[10], (DEPTH, D_FF, d), f32) * D_FF ** -0.5,
        'hgrn_w_in': nrm(ks[11], (N_REC_LAYERS, d, 5 * d), f32) * d ** -0.5,
        'hgrn_lb': nrm(ks[12], (2, N_REC_LAYERS, d), f32),
        'hgrn_gnorm': 1.0 + 0.02 * nrm(ks[13], (N_REC_LAYERS, d), f32),
        'hgrn_w_out': nrm(ks[14], (N_REC_LAYERS, d, d), f32) * d ** -0.5,
        'conv_w_in': nrm(ks[15], (N_CONV_LAYERS, d, 3 * d), f32) * d ** -0.5,
        'conv_w': nrm(ks[16], (N_CONV_LAYERS, CONV_WIDTH, d), f32) * CONV_WIDTH ** -0.5,
        'conv_b': 0.02 * nrm(ks[17], (N_CONV_LAYERS, d), f32),
        'conv_w_out': nrm(ks[18], (N_CONV_LAYERS, d, d), f32) * d ** -0.5,
    }


def reference(x, c, ctx, c_ctx, ada_w, ada_b, norm1, norm2, norm_f, mlp_w1, mlp_w2,
              hgrn_w_in, hgrn_lb, hgrn_gnorm, hgrn_w_out, conv_w_in, conv_w, conv_b, conv_w_out):
    bsz, seq, _ = x.shape
    rows = seq // GRID_W
    lb_p = jax.nn.softmax(hgrn_lb.astype(jnp.float32), axis=1)
    lower_bounds = jnp.cumsum(lb_p, axis=1) - lb_p[:, :1]
    silu_c = jax.nn.silu(c)
    silu_cc = jax.nn.silu(c_ctx)
    last_rec = ((DEPTH - 1) // N_MIXERS) * N_MIXERS
    x_ctx = ctx
    for i in range(DEPTH):
        j = i // N_MIXERS
        recurrent = i % N_MIXERS == 0
        ctx_live = i < last_rec
        sh1, sc1, g1, sh2, sc2, g2 = jnp.split((silu_c @ ada_w[i] + ada_b[i])[:, None, :], 6, axis=-1)
        h = _modulate(_rmsnorm(x, norm1[i]), sh1, sc1)
        if ctx_live or recurrent:
            csh1, csc1, cg1, csh2, csc2, cg2 = jnp.split(silu_cc @ ada_w[i] + ada_b[i], 6)
            hc = _modulate(_rmsnorm(x_ctx, norm1[i]), csh1, csc1)
        if recurrent:
            y, yc = _hgrn2_mixer(h, hc, hgrn_w_in[j], lower_bounds[:, j], hgrn_gnorm[j],
                                 hgrn_w_out[j], ctx_live)
        else:
            axis = 2 if j % 2 == 0 else 1
            y = _shortconv_mixer(h.reshape(bsz, rows, GRID_W, D_MODEL), conv_w_in[j], conv_w[j],
                                 conv_b[j], conv_w_out[j], axis).reshape(bsz, seq, D_MODEL)
            yc = _shortconv_mixer(hc, conv_w_in[j], conv_w[j], conv_b[j], conv_w_out[j], 1) if ctx_live else None
        x = x + g1 * y
        x = x + g2 * _mlp(_modulate(_rmsnorm(x, norm2[i]), sh2, sc2), mlp_w1[i], mlp_w2[i])
        if ctx_live:
            x_ctx = x_ctx + cg1 * yc
            x_ctx = x_ctx + cg2 * _mlp(_modulate(_rmsnorm(x_ctx, norm2[i]), csh2, csc2), mlp_w1[i], mlp_w2[i])
    return _rmsnorm(x, norm_f)
```

```python
import functools

import jax
import jax.numpy as jnp
from jax import lax
from jax.experimental import pallas as pl
from jax.experimental.pallas import tpu as pltpu

F32 = jnp.float32
BF16 = jnp.bfloat16

GRID_W = 64
HEAD_DIM = 128
CHUNK = 64
EPS = 1e-6
N_MIXERS = 2
CTX_MOD_ROW = 4
MOD_ROWS = 8
N_MODS = 6

VMEM_LIMIT = 56 * 1024 * 1024

TM_PROJ = 256
TM_MLP = 512
TM_CONV = 512
T_SCAN = 256
TN_ADA = 2048


def _cparams(sem):
    return pltpu.CompilerParams(dimension_semantics=sem, vmem_limit_bytes=VMEM_LIMIT)


def _const_spec(shape, index_map):
    return pl.BlockSpec(shape, index_map, pipeline_mode=pl.Buffered(1))


def _silu(t):
    return t * jax.nn.sigmoid(t)


def _rms_mod(x, gain, shift, scale):
    y = x * lax.rsqrt(jnp.mean(x * x, axis=-1, keepdims=True) + EPS) * gain
    return y * (1.0 + scale) + shift


def _ada_kernel(c_ref, w_ref, b_ref, o_ref):
    s = _silu(c_ref[...]).astype(BF16)
    o_ref[...] = jnp.dot(s, w_ref[...].astype(BF16), preferred_element_type=F32) + b_ref[...]


def _ada_table(c8, ada_w, ada_b):
    depth, d, n = ada_w.shape
    return pl.pallas_call(
        _ada_kernel,
        grid=(depth, n // TN_ADA),
        in_specs=[
            pl.BlockSpec((MOD_ROWS, d), lambda l, j: (0, 0)),
            pl.BlockSpec((None, d, TN_ADA), lambda l, j: (l, 0, j)),
            pl.BlockSpec((None, 1, TN_ADA), lambda l, j: (l, 0, j)),
        ],
        out_specs=pl.BlockSpec((None, MOD_ROWS, TN_ADA), lambda l, j: (l, 0, j)),
        out_shape=jax.ShapeDtypeStruct((depth, MOD_ROWS, n), F32),
        compiler_params=_cparams(("parallel", "parallel")),
        name="ada_table",
    )(c8, ada_w, ada_b.reshape(depth, 1, n))


def _lb_kernel(x_ref, o_ref):
    n_rec = x_ref.shape[1]
    for d in range(x_ref.shape[0]):
        rows = [x_ref[d, j:j + 1, :] for j in range(n_rec)]
        m = functools.reduce(jnp.maximum, rows)
        e = [jnp.exp(r - m) for r in rows]
        tot = functools.reduce(lambda a, b: a + b, e)
        acc = None
        for j in range(n_rec):
            p = e[j] / tot
            acc = p if acc is None else acc + p
            o_ref[d, j:j + 1, :] = acc - e[0] / tot


def _lower_bounds(hgrn_lb):
    return pl.pallas_call(
        _lb_kernel,
        out_shape=jax.ShapeDtypeStruct(hgrn_lb.shape, F32),
        name="hgrn_lower_bounds",
    )(hgrn_lb)


class _Rows:
    def __init__(self, bsz, seq, ctx_len, d):
        self.bsz, self.seq, self.ctx_len, self.d = bsz, seq, ctx_len, d
        self.n_ctx = bsz * ctx_len
        self.n_lat = bsz * seq
        self.n_rows = self.n_ctx + self.n_lat

    def mod_row(self, tm):
        n_ctx_tiles = self.n_ctx // tm
        per_batch = self.seq // tm
        return lambda i: jnp.where(i < n_ctx_tiles, CTX_MOD_ROW, (i - n_ctx_tiles) // per_batch)


def _mod_block(d, layer, k, row_of_grid):
    return pl.BlockSpec((None, 1, d),
                        lambda *g: ((layer * MOD_ROWS + row_of_grid(*g)) * N_MODS + k, 0, 0))


def _layer_vec_spec(d, layer):
    return pl.BlockSpec((None, 1, d), lambda *g: (layer, 0, 0))


def _proj_kernel(x_ref, gain_ref, sh_ref, sc_ref, w_ref, o_ref):
    h = _rms_mod(x_ref[...], gain_ref[...], sh_ref[...], sc_ref[...])
    o_ref[...] = jnp.dot(h.astype(BF16), w_ref[...], preferred_element_type=F32)


def _hgrn_proj(rows, x2, norm1, mods, w_in, layer, j):
    d = rows.d
    n = w_in.shape[-1]
    tm = TM_PROJ
    row = rows.mod_row(tm)
    return pl.pallas_call(
        _proj_kernel,
        grid=(rows.n_rows // tm,),
        in_specs=[
            pl.BlockSpec((tm, d), lambda i: (i, 0)),
            _layer_vec_spec(d, layer),
            _mod_block(d, layer, 0, row),
            _mod_block(d, layer, 1, row),
            _const_spec((None, d, n), lambda i: (j, 0, 0)),
        ],
        out_specs=pl.BlockSpec((tm, n), lambda i: (i, 0)),
        out_shape=jax.ShapeDtypeStruct((rows.n_rows, n), F32),
        compiler_params=_cparams(("parallel",)),
        name="hgrn_proj",
    )(x2, norm1, mods, mods, w_in)


def _cumsum_rows(g, reverse):
    n = g.shape[0]
    row = lax.broadcasted_iota(jnp.int32, g.shape, 0)
    s = 1
    while s < n:
        if reverse:
            g = g + jnp.where(row < n - s, pltpu.roll(g, n - s, axis=0), 0.0)
        else:
            g = g + jnp.where(row >= s, pltpu.roll(g, s, axis=0), 0.0)
        s *= 2
    return g


def _gla_tile(z_ref, q_ref, v_ref, lb, s_ref, reverse, emit):
    t_rows, d = z_ref.shape
    n_heads = d // HEAD_DIM
    n_chunks = t_rows // CHUNK
    r = lax.broadcasted_iota(jnp.int32, (CHUNK, CHUNK), 0)
    c = lax.broadcasted_iota(jnp.int32, (CHUNK, CHUNK), 1)
    mask = (c >= r) if reverse else (c <= r)
    nt = (((1,), (1,)), ((), ()))
    tn = (((0,), (0,)), ((), ()))
    order = range(n_chunks - 1, -1, -1) if reverse else range(n_chunks)
    for ci in order:
        rows = slice(ci * CHUNK, (ci + 1) * CHUNK)
        f = lb + (1.0 - lb) * jax.nn.sigmoid(z_ref[rows, :])
        k = 1.0 - f
        cum = _cumsum_rows(jnp.log(f), reverse)
        if reverse:
            ref, last = cum[CHUNK // 2:CHUNK // 2 + 1], cum[0:1]
        else:
            ref, last = cum[CHUNK // 2 - 1:CHUNK // 2], cum[CHUNK - 1:CHUNK]
        q = _silu(q_ref[rows, :])
        qd = (q * jnp.exp(cum)).astype(BF16)
        qs = (q * jnp.exp(cum - ref)).astype(BF16)
        ks = (k * jnp.exp(ref - cum)).astype(BF16)
        kl = (k * jnp.exp(last - cum)).astype(BF16)
        el = jnp.exp(last)
        vb = v_ref[rows, :].astype(BF16)
        for h in range(n_heads):
            hs = slice(h * HEAD_DIM, (h + 1) * HEAD_DIM)
            st = s_ref[h]
            o_inter = lax.dot_general(qd[:, hs], st.astype(BF16), nt, preferred_element_type=F32)
            sc = lax.dot_general(qs[:, hs], ks[:, hs], nt, preferred_element_type=F32)
            sc = jnp.where(mask, sc, 0.0).astype(BF16)
            o_intra = jnp.dot(sc, vb[:, hs], preferred_element_type=F32)
            s_ref[h] = st * el[:, hs] + lax.dot_general(vb[:, hs], kl[:, hs], tn,
                                                        preferred_element_type=F32)
            emit(rows, hs, o_inter + o_intra)


def _scan_bwd_kernel(z_ref, v_ref, q_ref, lb_ref, o_ref, s_ref):
    @pl.when(pl.program_id(1) == 0)
    def _():
        s_ref[...] = jnp.zeros_like(s_ref)

    def emit(rows, hs, o):
        o_ref[rows, hs] = o

    _gla_tile(z_ref, q_ref, v_ref, lb_ref[...], s_ref, True, emit)


def _scan_fwd_kernel(z_ref, v_ref, q_ref, gate_ref, ob_ref, x_ref, lb_ref, gn_ref, g1_ref,
                     wout_ref, o_ref, s_ref, y_ref):
    @pl.when(pl.program_id(1) == 0)
    def _():
        s_ref[...] = jnp.zeros_like(s_ref)

    def emit(rows, hs, o):
        o = o + ob_ref[rows, hs]
        o = o * lax.rsqrt(jnp.mean(o * o, axis=-1, keepdims=True) + EPS)
        y_ref[rows, hs] = (o * gn_ref[:, hs] * _silu(gate_ref[rows, hs])).astype(BF16)

    _gla_tile(z_ref, q_ref, v_ref, lb_ref[...], s_ref, False, emit)
    y = jnp.dot(y_ref[...], wout_ref[...], preferred_element_type=F32)
    o_ref[...] = x_ref[...] + g1_ref[...] * y


def _scan_tiles(rows, reverse):
    t = T_SCAN
    nct = rows.ctx_len // t
    ntl = rows.seq // t
    ctx_tiles = rows.n_ctx // t

    def tile(b, s):
        if reverse:
            return jnp.where(s < nct, b * nct + (nct - 1 - s), ctx_tiles + b * ntl + (ntl - 1 - (s - nct)))
        return jnp.where(s < nct, b * nct + s, ctx_tiles + b * ntl + (s - nct))

    mod_row = lambda b, s: jnp.where(s < nct, CTX_MOD_ROW, b)
    return tile, mod_row, nct + ntl


def _hgrn_scan_bwd(rows, proj, lb, j):
    d = rows.d
    t = T_SCAN
    tile, _, steps = _scan_tiles(rows, True)
    n_heads = d // HEAD_DIM
    part = lambda p: pl.BlockSpec((t, d), lambda b, s: (tile(b, s), p))
    return pl.pallas_call(
        _scan_bwd_kernel,
        grid=(rows.bsz, steps),
        in_specs=[part(1), part(2), part(3),
                  pl.BlockSpec((None, None, 1, d), lambda b, s: (1, j, 0, 0))],
        out_specs=pl.BlockSpec((t, d), lambda b, s: (tile(b, s), 0)),
        out_shape=jax.ShapeDtypeStruct((rows.n_rows, d), F32),
        scratch_shapes=[pltpu.VMEM((n_heads, HEAD_DIM, HEAD_DIM), F32)],
        compiler_params=_cparams(("arbitrary", "arbitrary")),
        name="hgrn_scan_bwd",
    )(proj, proj, proj, lb)


def _hgrn_scan_fwd(rows, proj, o_bwd, x2, lb, gnorm, mods, w_out, layer, j):
    d = rows.d
    t = T_SCAN
    tile, mod_row, steps = _scan_tiles(rows, False)
    n_heads = d // HEAD_DIM
    part = lambda p: pl.BlockSpec((t, d), lambda b, s: (tile(b, s), p))
    return pl.pallas_call(
        _scan_fwd_kernel,
        grid=(rows.bsz, steps),
        in_specs=[part(0), part(2), part(3), part(4), part(0), part(0),
                  pl.BlockSpec((None, None, 1, d), lambda b, s: (0, j, 0, 0)),
                  _layer_vec_spec(d, j),
                  _mod_block(d, layer, 2, mod_row),
                  _const_spec((None, d, d), lambda b, s: (j, 0, 0))],
        out_specs=pl.BlockSpec((t, d), lambda b, s: (tile(b, s), 0)),
        out_shape=jax.ShapeDtypeStruct((rows.n_rows, d), F32),
        scratch_shapes=[pltpu.VMEM((n_heads, HEAD_DIM, HEAD_DIM), F32),
                        pltpu.VMEM((t, d), BF16)],
        compiler_params=_cparams(("arbitrary", "arbitrary")),
        name="hgrn_scan_fwd",
    )(proj, proj, proj, proj, o_bwd, x2, lb, gnorm, mods, w_out)


def _conv_taps(cw_ref, cb_ref, um, u, up):
    return cb_ref[...] + cw_ref[0:1, :] * um + cw_ref[1:2, :] * u + cw_ref[2:3, :] * up


def _conv_seq_kernel(n_ctx_tiles, ctx_len, x_ref, gain_ref, sh_ref, sc_ref, g1_ref, win_ref,
                     cw_ref, cb_ref, wout_ref, o_ref):
    x = x_ref[...]
    tm, d = x.shape
    h = _rms_mod(x, gain_ref[...], sh_ref[...], sc_ref[...]).astype(BF16)
    p = jnp.dot(h, win_ref[...], preferred_element_type=F32)
    gate_b, u = p[:, :d], p[:, d:2 * d] * p[:, 2 * d:]
    period = jnp.where(pl.program_id(0) < n_ctx_tiles, ctx_len, GRID_W)
    pos = lax.broadcasted_iota(jnp.int32, (tm, 1), 0) & (period - 1)
    um = jnp.where(pos == 0, 0.0, pltpu.roll(u, 1, axis=0))
    up = jnp.where(pos == period - 1, 0.0, pltpu.roll(u, tm - 1, axis=0))
    cv = _conv_taps(cw_ref, cb_ref, um, u, up)
    y = jnp.dot((gate_b * cv).astype(BF16), wout_ref[...], preferred_element_type=F32)
    o_ref[...] = x + g1_ref[...] * y


def _conv_rows_kernel(n_ctx_tiles, tiles_per_batch, xp_ref, x_ref, xn_ref, gain_ref, sh_ref,
                      sc_ref, g1_ref, win_ref, cw_ref, cb_ref, wout_ref, o_ref, h_ref):
    x = x_ref[...]
    tm, d = x.shape
    gain, sh, sc = gain_ref[...], sh_ref[...], sc_ref[...]
    h_ref[0:GRID_W, :] = _rms_mod(xp_ref[...], gain, sh, sc).astype(BF16)
    h_ref[GRID_W:GRID_W + tm, :] = _rms_mod(x, gain, sh, sc).astype(BF16)
    h_ref[GRID_W + tm:, :] = _rms_mod(xn_ref[...], gain, sh, sc).astype(BF16)
    p = jnp.dot(h_ref[...], win_ref[:, d:], preferred_element_type=F32)
    u = p[:, :d] * p[:, d:]
    gate_b = jnp.dot(h_ref[GRID_W:GRID_W + tm, :], win_ref[:, :d], preferred_element_type=F32)
    i = pl.program_id(0)
    jl = (i - n_ctx_tiles) % tiles_per_batch
    is_lat = i >= n_ctx_tiles
    has_prev = jnp.logical_and(is_lat, jl != 0)
    has_next = jnp.logical_and(is_lat, jl != tiles_per_batch - 1)
    row = lax.broadcasted_iota(jnp.int32, (tm, 1), 0)
    um = jnp.where(jnp.logical_or(row >= GRID_W, has_prev), u[0:tm], 0.0)
    up = jnp.where(jnp.logical_or(row < tm - GRID_W, has_next), u[2 * GRID_W:], 0.0)
    cv = _conv_taps(cw_ref, cb_ref, um, u[GRID_W:GRID_W + tm], up)
    y = jnp.dot((gate_b * cv).astype(BF16), wout_ref[...], preferred_element_type=F32)
    o_ref[...] = x + g1_ref[...] * y


def _conv_mixer(rows, x2, norm1, mods, w_in, cw, cb, w_out, layer, j, along_rows):
    d = rows.d
    tm = TM_CONV
    row = rows.mod_row(tm)
    n_tiles = rows.n_rows // tm
    n_ctx_tiles = rows.n_ctx // tm
    common = [
        _layer_vec_spec(d, layer),
        _mod_block(d, layer, 0, row),
        _mod_block(d, layer, 1, row),
        _mod_block(d, layer, 2, row),
        _const_spec((None, d, 3 * d), lambda i: (j, 0, 0)),
        pl.BlockSpec((None, 3, d), lambda i: (j, 0, 0)),
        _layer_vec_spec(d, j),
        _const_spec((None, d, d), lambda i: (j, 0, 0)),
    ]
    args = (norm1, mods, mods, mods, w_in, cw, cb, w_out)
    if along_rows:
        sub = tm // GRID_W
        last = rows.n_rows // GRID_W - 1
        kern = functools.partial(_conv_rows_kernel, n_ctx_tiles, rows.seq // tm)
        in_specs = [pl.BlockSpec((GRID_W, d), lambda i: (jnp.maximum(i * sub - 1, 0), 0)),
                    pl.BlockSpec((tm, d), lambda i: (i, 0)),
                    pl.BlockSpec((GRID_W, d), lambda i: (jnp.minimum((i + 1) * sub, last), 0))]
        args = (x2, x2, x2) + args
        scratch = [pltpu.VMEM((tm + 2 * GRID_W, d), BF16)]
        name = "conv_mixer_rows"
    else:
        kern = functools.partial(_conv_seq_kernel, n_ctx_tiles, rows.ctx_len)
        in_specs = [pl.BlockSpec((tm, d), lambda i: (i, 0))]
        args = (x2,) + args
        scratch = []
        name = "conv_mixer_seq"
    return pl.pallas_call(
        kern,
        grid=(n_tiles,),
        in_specs=in_specs + common,
        out_specs=pl.BlockSpec((tm, d), lambda i: (i, 0)),
        out_shape=jax.ShapeDtypeStruct((rows.n_rows, d), F32),
        scratch_shapes=scratch,
        compiler_params=_cparams(("parallel",)),
        name=name,
    )(*args)


def _mlp_kernel(x_ref, gain_ref, sh_ref, sc_ref, g2_ref, w1_ref, w2_ref, o_ref):
    x = x_ref[...]
    h = _rms_mod(x, gain_ref[...], sh_ref[...], sc_ref[...]).astype(BF16)
    a = jnp.dot(h, w1_ref[...], preferred_element_type=F32)
    a = jnp.square(jnp.maximum(a, 0.0)).astype(BF16)
    y = jnp.dot(a, w2_ref[...], preferred_element_type=F32)
    o_ref[...] = x + g2_ref[...] * y


def _mlp(rows, x2, norm2, mods, w1, w2, layer):
    d = rows.d
    d_ff = w1.shape[-1]
    tm = TM_MLP
    row = rows.mod_row(tm)
    return pl.pallas_call(
        _mlp_kernel,
        grid=(rows.n_rows // tm,),
        in_specs=[
            pl.BlockSpec((tm, d), lambda i: (i, 0)),
            _layer_vec_spec(d, layer),
            _mod_block(d, layer, 3, row),
            _mod_block(d, layer, 4, row),
            _mod_block(d, layer, 5, row),
            _const_spec((None, d, d_ff), lambda i: (layer, 0, 0)),
            _const_spec((None, d_ff, d), lambda i: (layer, 0, 0)),
        ],
        out_specs=pl.BlockSpec((tm, d), lambda i: (i, 0)),
        out_shape=jax.ShapeDtypeStruct((rows.n_rows, d), F32),
        compiler_params=_cparams(("parallel",)),
        name="mlp",
    )(x2, norm2, mods, mods, mods, w1, w2)


def _final_norm_kernel(x_ref, gain_ref, o_ref):
    x = x_ref[...]
    o_ref[...] = x * lax.rsqrt(jnp.mean(x * x, axis=-1, keepdims=True) + EPS) * gain_ref[...]


def _final_norm(rows, x2, norm_f):
    d = rows.d
    tm = TM_MLP
    n_ctx_tiles = rows.n_ctx // tm
    return pl.pallas_call(
        _final_norm_kernel,
        grid=(rows.n_lat // tm,),
        in_specs=[pl.BlockSpec((tm, d), lambda i: (i + n_ctx_tiles, 0)),
                  pl.BlockSpec((1, d), lambda i: (0, 0))],
        out_specs=pl.BlockSpec((tm, d), lambda i: (i, 0)),
        out_shape=jax.ShapeDtypeStruct((rows.n_lat, d), F32),
        compiler_params=_cparams(("parallel",)),
        name="final_norm",
    )(x2, norm_f.reshape(1, d))


def kernel(x, c, ctx, c_ctx, ada_w, ada_b, norm1, norm2, norm_f, mlp_w1, mlp_w2, hgrn_w_in, hgrn_lb,
           hgrn_gnorm, hgrn_w_out, conv_w_in, conv_w, conv_b, conv_w_out):
    bsz, seq, d = x.shape
    ctx_len = ctx.shape[1]
    depth = ada_w.shape[0]
    assert bsz <= CTX_MOD_ROW and d % HEAD_DIM == 0
    assert seq % max(TM_MLP, TM_CONV, T_SCAN) == 0 and ctx_len % T_SCAN == 0
    assert (bsz * ctx_len) % max(TM_MLP, TM_CONV) == 0
    assert ctx_len & (ctx_len - 1) == 0 and TM_CONV % ctx_len == 0
    rows = _Rows(bsz, seq, ctx_len, d)

    c8 = jnp.concatenate([c, c_ctx[None, :], jnp.zeros((MOD_ROWS - bsz - 1, d), F32)], axis=0)
    mods = _ada_table(c8, ada_w, ada_b).reshape(depth * MOD_ROWS * N_MODS, 1, d)
    lb = _lower_bounds(hgrn_lb).reshape(2, -1, 1, d)

    norm1 = norm1.reshape(depth, 1, d)
    norm2 = norm2.reshape(depth, 1, d)
    gnorm = hgrn_gnorm.reshape(-1, 1, d)
    conv_b = conv_b.reshape(-1, 1, d)
    w1, w2 = mlp_w1.astype(BF16), mlp_w2.astype(BF16)
    hw_in, hw_out = hgrn_w_in.astype(BF16), hgrn_w_out.astype(BF16)
    cw_in, cw_out = conv_w_in.astype(BF16), conv_w_out.astype(BF16)

    x2 = jnp.concatenate([ctx.reshape(bsz * ctx_len, d), x.reshape(bsz * seq, d)], axis=0)
    for i in range(depth):
        j = i // N_MIXERS
        if i % N_MIXERS == 0:
            proj = _hgrn_proj(rows, x2, norm1, mods, hw_in, i, j)
            o_bwd = _hgrn_scan_bwd(rows, proj, lb, j)
            x2 = _hgrn_scan_fwd(rows, proj, o_bwd, x2, lb, gnorm, mods, hw_out, i, j)
        else:
            x2 = _conv_mixer(rows, x2, norm1, mods, cw_in, conv_w, conv_b, cw_out, i, j,
                             along_rows=(j % 2 == 1))
        x2 = _mlp(rows, x2, norm2, mods, w1, w2, i)
    return _final_norm(rows, x2, norm_f).reshape(bsz, seq, d)
```

```python
import functools

import jax
import jax.numpy as jnp
from jax import lax
from jax.experimental import pallas as pl
from jax.experimental.pallas import tpu as pltpu

F32 = jnp.float32
BF16 = jnp.bfloat16

GRID_W = 64
HEAD_DIM = 128
CHUNK = 64
EPS = 1e-6
N_MIXERS = 2
CTX_MOD_ROW = 4
MOD_ROWS = 8
N_MODS = 6
DEC_ROWS = 8

VMEM_LIMIT = 56 * 1024 * 1024

TM_PREP = 512
TM_MLP = 512
TM_CONV = 512
T_SCAN = 256
TN_ADA = 2048


def _cparams(sem):
    return pltpu.CompilerParams(dimension_semantics=sem, vmem_limit_bytes=VMEM_LIMIT)


def _const_spec(shape, index_map):
    return pl.BlockSpec(shape, index_map, pipeline_mode=pl.Buffered(1))


def _silu(t):
    return t * jax.nn.sigmoid(t)


def _rms_mod(x, gain, shift, scale):
    y = x * lax.rsqrt(jnp.mean(x * x, axis=-1, keepdims=True) + EPS) * gain
    return y * (1.0 + scale) + shift


def _ada_kernel(c_ref, w_ref, b_ref, o_ref):
    s = _silu(c_ref[...]).astype(BF16)
    o_ref[...] = jnp.dot(s, w_ref[...].astype(BF16), preferred_element_type=F32) + b_ref[...]


def _ada_table(c8, ada_w, ada_b):
    depth, d, n = ada_w.shape
    return pl.pallas_call(
        _ada_kernel,
        grid=(depth, n // TN_ADA),
        in_specs=[
            pl.BlockSpec((MOD_ROWS, d), lambda l, j: (0, 0)),
            pl.BlockSpec((None, d, TN_ADA), lambda l, j: (l, 0, j)),
            pl.BlockSpec((None, 1, TN_ADA), lambda l, j: (l, 0, j)),
        ],
        out_specs=pl.BlockSpec((None, MOD_ROWS, TN_ADA), lambda l, j: (l, 0, j)),
        out_shape=jax.ShapeDtypeStruct((depth, MOD_ROWS, n), F32),
        compiler_params=_cparams(("parallel", "parallel")),
        name="ada_table",
    )(c8, ada_w, ada_b.reshape(depth, 1, n))


def _lb_kernel(x_ref, o_ref):
    n_rec = x_ref.shape[1]
    for d in range(x_ref.shape[0]):
        rows = [x_ref[d, j:j + 1, :] for j in range(n_rec)]
        m = functools.reduce(jnp.maximum, rows)
        e = [jnp.exp(r - m) for r in rows]
        tot = functools.reduce(lambda a, b: a + b, e)
        acc = None
        for j in range(n_rec):
            p = e[j] / tot
            acc = p if acc is None else acc + p
            o_ref[d, j:j + 1, :] = acc - e[0] / tot


def _lower_bounds(hgrn_lb):
    return pl.pallas_call(
        _lb_kernel,
        out_shape=jax.ShapeDtypeStruct(hgrn_lb.shape, F32),
        name="hgrn_lower_bounds",
    )(hgrn_lb)


class _Rows:
    def __init__(self, bsz, seq, ctx_len, d):
        self.bsz, self.seq, self.ctx_len, self.d = bsz, seq, ctx_len, d
        self.n_ctx = bsz * ctx_len
        self.n_lat = bsz * seq
        self.n_rows = self.n_ctx + self.n_lat

    def mod_row(self, tm):
        n_ctx_tiles = self.n_ctx // tm
        per_batch = self.seq // tm
        return lambda i: jnp.where(i < n_ctx_tiles, CTX_MOD_ROW, (i - n_ctx_tiles) // per_batch)


def _mod_block(d, layer, k, row_of_grid):
    return pl.BlockSpec((None, 1, d),
                        lambda *g: ((layer * MOD_ROWS + row_of_grid(*g)) * N_MODS + k, 0, 0))


def _layer_vec_spec(d, layer):
    return pl.BlockSpec((None, 1, d), lambda *g: (layer, 0, 0))


SUB = 8


def _chunk_view(a):
    n, d = a.shape
    return a.reshape(n // CHUNK, SUB, SUB * d)


def _load_interleaved(ref, ci):
    d = ref.shape[-1] // SUB
    return jnp.concatenate([ref[ci, :, j * d:(j + 1) * d] for j in range(SUB)], axis=0)


def _store_interleaved(ref, ci, val):
    d = ref.shape[-1] // SUB
    for j in range(SUB):
        ref[ci, :, j * d:(j + 1) * d] = val[SUB * j:SUB * (j + 1)]


def _interleaved_row(t):
    return SUB * (t % SUB) + t // SUB


def _cumsum_interleaved(g, reverse):
    slabs = [g[SUB * j:SUB * (j + 1)] for j in range(SUB)]
    order = range(SUB - 1, -1, -1) if reverse else range(SUB)
    acc, run = None, {}
    for j in order:
        acc = slabs[j] if acc is None else acc + slabs[j]
        run[j] = acc
    row = lax.broadcasted_iota(jnp.int32, acc.shape, 0)
    inc, s = acc, 1
    while s < SUB:
        if reverse:
            inc = inc + jnp.where(row < SUB - s, pltpu.roll(inc, SUB - s, axis=0), 0.0)
        else:
            inc = inc + jnp.where(row >= s, pltpu.roll(inc, s, axis=0), 0.0)
        s *= 2
    exc = inc - acc
    return jnp.concatenate([run[j] + exc for j in range(SUB)], axis=0)


def _prep_kernel(x_ref, gain_ref, sh_ref, sc_ref, w_ref, lbf_ref, lbb_ref,
                 qsf_ref, ksf_ref, qsb_ref, ksb_ref, v_ref, gate_ref, dec_ref, z_ref, h_ref):
    tm, d = v_ref.shape
    step = pl.program_id(0)
    wr = step % 2
    rd = 1 - wr

    @pl.when(step == 0)
    def _():
        z_ref[1] = jnp.zeros(z_ref.shape[1:], F32)

    n_chunks = tm // CHUNK
    gain, shift, scale = gain_ref[...], sh_ref[...], sc_ref[...]
    for ci in range(n_chunks):
        h_ref[ci * CHUNK:(ci + 1) * CHUNK, :] = _rms_mod(
            _load_interleaved(x_ref, ci), gain, shift, scale).astype(BF16)
    wp = d // 2
    n_pieces = w_ref.shape[1] // wp

    def project(piece):
        cols = slice(piece * wp, (piece + 1) * wp)
        res = jnp.dot(h_ref[...], w_ref[:, cols], preferred_element_type=F32)
        p, off = divmod(piece * wp, d)
        dst = slice(off, off + wp)
        if p == 2:
            v_ref[:, dst] = res.astype(BF16)
        elif p == 4:
            gate_ref[:, dst] = _silu(res).astype(BF16)
        else:
            zc = {0: 0, 1: d, 3: 2 * d}[p] + off
            z_ref[wr, :, zc:zc + wp] = res

    dec_ref[:, 4:, :] = jnp.zeros((n_chunks, DEC_ROWS - 4, d), F32)
    for ci in range(n_chunks):
        for piece in range(ci * n_pieces // n_chunks, (ci + 1) * n_pieces // n_chunks):
            project(piece)
        rows = slice(ci * CHUNK, (ci + 1) * CHUNK)
        q = _silu(z_ref[rd, rows, 2 * d:3 * d])
        for dirn, (lb_ref, qs_ref, ks_ref) in enumerate(((lbf_ref, qsf_ref, ksf_ref),
                                                         (lbb_ref, qsb_ref, ksb_ref))):
            lb = lb_ref[...]
            f = lb + (1.0 - lb) * jax.nn.sigmoid(z_ref[rd, rows, dirn * d:(dirn + 1) * d])
            cum = _cumsum_interleaved(jnp.log(f), dirn == 1)
            r_ref = _interleaved_row(CHUNK // 2 if dirn == 1 else CHUNK // 2 - 1)
            r_last = _interleaved_row(0 if dirn == 1 else CHUNK - 1)
            ref, last = cum[r_ref:r_ref + 1], cum[r_last:r_last + 1]
            qs_ref[rows, :] = (q * jnp.exp(cum - ref)).astype(BF16)
            ks_ref[rows, :] = ((1.0 - f) * jnp.exp(ref - cum)).astype(BF16)
            dec_ref[ci, 2 * dirn:2 * dirn + 1, :] = jnp.exp(ref)
            dec_ref[ci, 2 * dirn + 1:2 * dirn + 2, :] = jnp.exp(last - ref)


def _hgrn_prep(rows, x2, norm1, mods, w_in, lb, layer, j):
    d = rows.d
    n = w_in.shape[-1]
    tm = TM_PREP
    n_tiles = rows.n_rows // tm
    cur = lambda s: jnp.minimum(s, n_tiles - 1)
    prev = lambda s: jnp.maximum(s - 1, 0)
    row = lambda s: rows.mod_row(tm)(cur(s))
    act_cur = pl.BlockSpec((tm, d), lambda s: (cur(s), 0))
    act_prev = pl.BlockSpec((tm, d), lambda s: (prev(s), 0))
    act_shape = jax.ShapeDtypeStruct((rows.n_rows, d), BF16)
    return pl.pallas_call(
        _prep_kernel,
        grid=(n_tiles + 1,),
        in_specs=[
            pl.BlockSpec((tm // CHUNK, SUB, SUB * d), lambda s: (cur(s), 0, 0)),
            _layer_vec_spec(d, layer),
            _mod_block(d, layer, 0, row),
            _mod_block(d, layer, 1, row),
            _const_spec((None, d, n), lambda s: (j, 0, 0)),
            pl.BlockSpec((None, None, 1, d), lambda s: (0, j, 0, 0)),
            pl.BlockSpec((None, None, 1, d), lambda s: (1, j, 0, 0)),
        ],
        out_specs=[act_prev] * 4 + [act_cur] * 2
        + [pl.BlockSpec((tm // CHUNK, DEC_ROWS, d), lambda s: (prev(s), 0, 0))],
        out_shape=[act_shape] * 6 + [jax.ShapeDtypeStruct((rows.n_rows // CHUNK, DEC_ROWS, d), F32)],
        scratch_shapes=[pltpu.VMEM((2, tm, 3 * d), F32), pltpu.VMEM((tm, d), BF16)],
        compiler_params=_cparams(("arbitrary",)),
        name="hgrn_prep",
    )(_chunk_view(x2), norm1, mods, mods, w_in, lb, lb)


def _gla_tile(qs_ref, ks_ref, v_ref, dec_ref, s_ref, dirn, emit):
    t_rows, d = qs_ref.shape
    reverse = dirn == 1
    n_chunks = t_rows // CHUNK
    time_of = lambda idx: SUB * (idx % SUB) + idx // SUB
    r = time_of(lax.broadcasted_iota(jnp.int32, (CHUNK, CHUNK), 0))
    c = time_of(lax.broadcasted_iota(jnp.int32, (CHUNK, CHUNK), 1))
    mask = (c >= r) if reverse else (c <= r)
    nt = (((1,), (1,)), ((), ()))
    tn = (((0,), (0,)), ((), ()))
    order = range(n_chunks - 1, -1, -1) if reverse else range(n_chunks)
    heads = range(d // HEAD_DIM)
    hsl = lambda h: slice(h * HEAD_DIM, (h + 1) * HEAD_DIM)
    rsl = lambda ci: slice(ci * CHUNK, (ci + 1) * CHUNK)
    p, u = {}, {}
    for h in heads:
        for ci in order:
            sc = lax.dot_general(qs_ref[rsl(ci), hsl(h)], ks_ref[rsl(ci), hsl(h)], nt,
                                 preferred_element_type=F32)
            p[h, ci] = jnp.where(mask, sc, 0.0).astype(BF16)
    for h in heads:
        for ci in order:
            u[h, ci] = lax.dot_general(v_ref[rsl(ci), hsl(h)], ks_ref[rsl(ci), hsl(h)], tn,
                                       preferred_element_type=F32)
    for h in heads:
        st = s_ref[h]
        for ci in order:
            e_ref = dec_ref[ci, 2 * dirn:2 * dirn + 1, hsl(h)]
            e_lr = dec_ref[ci, 2 * dirn + 1:2 * dirn + 2, hsl(h)]
            o = lax.dot_general(qs_ref[rsl(ci), hsl(h)], (st * e_ref).astype(BF16), nt,
                                preferred_element_type=F32)
            o = o + jnp.dot(p[h, ci], v_ref[rsl(ci), hsl(h)], preferred_element_type=F32)
            st = st * (e_ref * e_lr) + u[h, ci] * e_lr
            emit(rsl(ci), hsl(h), o)
        s_ref[h] = st


def _scan_bwd_kernel(qs_ref, ks_ref, v_ref, dec_ref, o_ref, s_ref):
    @pl.when(pl.program_id(1) == 0)
    def _():
        s_ref[...] = jnp.zeros_like(s_ref)

    def emit(rows, hs, o):
        o_ref[rows, hs] = o

    _gla_tile(qs_ref, ks_ref, v_ref, dec_ref, s_ref, 1, emit)


def _scan_fwd_kernel(qs_ref, ks_ref, v_ref, dec_ref, gate_ref, ob_ref, x_ref, gn_ref, g1_ref,
                     wout_ref, o_ref, s_ref, y_ref):
    @pl.when(pl.program_id(1) == 0)
    def _():
        s_ref[...] = jnp.zeros_like(s_ref)

    def emit(rows, hs, o):
        o = o + ob_ref[rows, hs]
        o = o * lax.rsqrt(jnp.mean(o * o, axis=-1, keepdims=True) + EPS)
        y_ref[rows, hs] = (o * gn_ref[:, hs] * gate_ref[rows, hs].astype(F32)).astype(BF16)

    _gla_tile(qs_ref, ks_ref, v_ref, dec_ref, s_ref, 0, emit)
    y = jnp.dot(y_ref[...], wout_ref[...], preferred_element_type=F32)
    g1 = g1_ref[...]
    for ci in range(y.shape[0] // CHUNK):
        out = _load_interleaved(x_ref, ci) + g1 * y[ci * CHUNK:(ci + 1) * CHUNK]
        _store_interleaved(o_ref, ci, out)


def _scan_tiles(rows, reverse):
    t = T_SCAN
    nct = rows.ctx_len // t
    ntl = rows.seq // t
    ctx_tiles = rows.n_ctx // t

    def tile(b, s):
        if reverse:
            return jnp.where(s < nct, b * nct + (nct - 1 - s), ctx_tiles + b * ntl + (ntl - 1 - (s - nct)))
        return jnp.where(s < nct, b * nct + s, ctx_tiles + b * ntl + (s - nct))

    mod_row = lambda b, s: jnp.where(s < nct, CTX_MOD_ROW, b)
    return tile, mod_row, nct + ntl


def _hgrn_scan_bwd(rows, qs, ks, v, dec):
    d = rows.d
    t = T_SCAN
    tile, _, steps = _scan_tiles(rows, True)
    act = pl.BlockSpec((t, d), lambda b, s: (tile(b, s), 0))
    return pl.pallas_call(
        _scan_bwd_kernel,
        grid=(rows.bsz, steps),
        in_specs=[act, act, act,
                  pl.BlockSpec((t // CHUNK, DEC_ROWS, d), lambda b, s: (tile(b, s), 0, 0))],
        out_specs=act,
        out_shape=jax.ShapeDtypeStruct((rows.n_rows, d), F32),
        scratch_shapes=[pltpu.VMEM((d // HEAD_DIM, HEAD_DIM, HEAD_DIM), F32)],
        compiler_params=_cparams(("arbitrary", "arbitrary")),
        name="hgrn_scan_bwd",
    )(qs, ks, v, dec)


def _hgrn_scan_fwd(rows, qs, ks, v, dec, gate, o_bwd, x2, gnorm, mods, w_out, layer, j):
    d = rows.d
    t = T_SCAN
    tile, mod_row, steps = _scan_tiles(rows, False)
    act = pl.BlockSpec((t, d), lambda b, s: (tile(b, s), 0))
    nat = pl.BlockSpec((t // CHUNK, SUB, SUB * d), lambda b, s: (tile(b, s), 0, 0))
    out = pl.pallas_call(
        _scan_fwd_kernel,
        grid=(rows.bsz, steps),
        in_specs=[act, act, act,
                  pl.BlockSpec((t // CHUNK, DEC_ROWS, d), lambda b, s: (tile(b, s), 0, 0)),
                  act, act, nat,
                  _layer_vec_spec(d, j),
                  _mod_block(d, layer, 2, mod_row),
                  _const_spec((None, d, d), lambda b, s: (j, 0, 0))],
        out_specs=nat,
        out_shape=jax.ShapeDtypeStruct((rows.n_rows // CHUNK, SUB, SUB * d), F32),
        scratch_shapes=[pltpu.VMEM((d // HEAD_DIM, HEAD_DIM, HEAD_DIM), F32),
                        pltpu.VMEM((t, d), BF16)],
        compiler_params=_cparams(("arbitrary", "arbitrary")),
        name="hgrn_scan_fwd",
    )(qs, ks, v, dec, gate, o_bwd, _chunk_view(x2), gnorm, mods, w_out)
    return out.reshape(rows.n_rows, d)


def _conv_taps(cw_ref, cb_ref, um, u, up):
    return cb_ref[...] + cw_ref[0:1, :] * um + cw_ref[1:2, :] * u + cw_ref[2:3, :] * up


def _conv_seq_kernel(n_ctx_tiles, ctx_len, x_ref, gain_ref, sh_ref, sc_ref, g1_ref, win_ref,
                     cw_ref, cb_ref, wout_ref, o_ref):
    x = x_ref[...]
    tm, d = x.shape
    h = _rms_mod(x, gain_ref[...], sh_ref[...], sc_ref[...]).astype(BF16)
    p = jnp.dot(h, win_ref[...], preferred_element_type=F32)
    gate_b, u = p[:, :d], p[:, d:2 * d] * p[:, 2 * d:]
    period = jnp.where(pl.program_id(0) < n_ctx_tiles, ctx_len, GRID_W)
    pos = lax.broadcasted_iota(jnp.int32, (tm, 1), 0) & (period - 1)
    um = jnp.where(pos == 0, 0.0, pltpu.roll(u, 1, axis=0))
    up = jnp.where(pos == period - 1, 0.0, pltpu.roll(u, tm - 1, axis=0))
    cv = _conv_taps(cw_ref, cb_ref, um, u, up)
    y = jnp.dot((gate_b * cv).astype(BF16), wout_ref[...], preferred_element_type=F32)
    o_ref[...] = x + g1_ref[...] * y


def _conv_rows_kernel(n_ctx_tiles, tiles_per_batch, xp_ref, x_ref, xn_ref, gain_ref, sh_ref,
                      sc_ref, g1_ref, win_ref, cw_ref, cb_ref, wout_ref, o_ref, h_ref):
    x = x_ref[...]
    tm, d = x.shape
    gain, sh, sc = gain_ref[...], sh_ref[...], sc_ref[...]
    h_ref[0:GRID_W, :] = _rms_mod(xp_ref[...], gain, sh, sc).astype(BF16)
    h_ref[GRID_W:GRID_W + tm, :] = _rms_mod(x, gain, sh, sc).astype(BF16)
    h_ref[GRID_W + tm:, :] = _rms_mod(xn_ref[...], gain, sh, sc).astype(BF16)
    p = jnp.dot(h_ref[...], win_ref[:, d:], preferred_element_type=F32)
    u = p[:, :d] * p[:, d:]
    gate_b = jnp.dot(h_ref[GRID_W:GRID_W + tm, :], win_ref[:, :d], preferred_element_type=F32)
    i = pl.program_id(0)
    jl = (i - n_ctx_tiles) % tiles_per_batch
    is_lat = i >= n_ctx_tiles
    has_prev = jnp.logical_and(is_lat, jl != 0)
    has_next = jnp.logical_and(is_lat, jl != tiles_per_batch - 1)
    row = lax.broadcasted_iota(jnp.int32, (tm, 1), 0)
    um = jnp.where(jnp.logical_or(row >= GRID_W, has_prev), u[0:tm], 0.0)
    up = jnp.where(jnp.logical_or(row < tm - GRID_W, has_next), u[2 * GRID_W:], 0.0)
    cv = _conv_taps(cw_ref, cb_ref, um, u[GRID_W:GRID_W + tm], up)
    y = jnp.dot((gate_b * cv).astype(BF16), wout_ref[...], preferred_element_type=F32)
    o_ref[...] = x + g1_ref[...] * y


def _conv_mixer(rows, x2, norm1, mods, w_in, cw, cb, w_out, layer, j, along_rows):
    d = rows.d
    tm = TM_CONV
    row = rows.mod_row(tm)
    n_tiles = rows.n_rows // tm
    n_ctx_tiles = rows.n_ctx // tm
    common = [
        _layer_vec_spec(d, layer),
        _mod_block(d, layer, 0, row),
        _mod_block(d, layer, 1, row),
        _mod_block(d, layer, 2, row),
        _const_spec((None, d, 3 * d), lambda i: (j, 0, 0)),
        pl.BlockSpec((None, 3, d), lambda i: (j, 0, 0)),
        _layer_vec_spec(d, j),
        _const_spec((None, d, d), lambda i: (j, 0, 0)),
    ]
    args = (norm1, mods, mods, mods, w_in, cw, cb, w_out)
    if along_rows:
        sub = tm // GRID_W
        last = rows.n_rows // GRID_W - 1
        kern = functools.partial(_conv_rows_kernel, n_ctx_tiles, rows.seq // tm)
        in_specs = [pl.BlockSpec((GRID_W, d), lambda i: (jnp.maximum(i * sub - 1, 0), 0)),
                    pl.BlockSpec((tm, d), lambda i: (i, 0)),
                    pl.BlockSpec((GRID_W, d), lambda i: (jnp.minimum((i + 1) * sub, last), 0))]
        args = (x2, x2, x2) + args
        scratch = [pltpu.VMEM((tm + 2 * GRID_W, d), BF16)]
        name = "conv_mixer_rows"
    else:
        kern = functools.partial(_conv_seq_kernel, n_ctx_tiles, rows.ctx_len)
        in_specs = [pl.BlockSpec((tm, d), lambda i: (i, 0))]
        args = (x2,) + args
        scratch = []
        name = "conv_mixer_seq"
    return pl.pallas_call(
        kern,
        grid=(n_tiles,),
        in_specs=in_specs + common,
        out_specs=pl.BlockSpec((tm, d), lambda i: (i, 0)),
        out_shape=jax.ShapeDtypeStruct((rows.n_rows, d), F32),
        scratch_shapes=scratch,
        compiler_params=_cparams(("parallel",)),
        name=name,
    )(*args)


def _mlp_kernel(x_ref, gain_ref, sh_ref, sc_ref, g2_ref, w1_ref, w2_ref, o_ref):
    x = x_ref[...]
    h = _rms_mod(x, gain_ref[...], sh_ref[...], sc_ref[...]).astype(BF16)
    a = jnp.dot(h, w1_ref[...], preferred_element_type=F32)
    a = jnp.square(jnp.maximum(a, 0.0)).astype(BF16)
    y = jnp.dot(a, w2_ref[...], preferred_element_type=F32)
    o_ref[...] = x + g2_ref[...] * y


def _mlp(rows, x2, norm2, mods, w1, w2, layer):
    d = rows.d
    d_ff = w1.shape[-1]
    tm = TM_MLP
    row = rows.mod_row(tm)
    return pl.pallas_call(
        _mlp_kernel,
        grid=(rows.n_rows // tm,),
        in_specs=[
            pl.BlockSpec((tm, d), lambda i: (i, 0)),
            _layer_vec_spec(d, layer),
            _mod_block(d, layer, 3, row),
            _mod_block(d, layer, 4, row),
            _mod_block(d, layer, 5, row),
            _const_spec((None, d, d_ff), lambda i: (layer, 0, 0)),
            _const_spec((None, d_ff, d), lambda i: (layer, 0, 0)),
        ],
        out_specs=pl.BlockSpec((tm, d), lambda i: (i, 0)),
        out_shape=jax.ShapeDtypeStruct((rows.n_rows, d), F32),
        compiler_params=_cparams(("parallel",)),
        name="mlp",
    )(x2, norm2, mods, mods, mods, w1, w2)


def _final_norm_kernel(x_ref, gain_ref, o_ref):
    x = x_ref[...]
    o_ref[...] = x * lax.rsqrt(jnp.mean(x * x, axis=-1, keepdims=True) + EPS) * gain_ref[...]


def _final_norm(rows, x2, norm_f):
    d = rows.d
    tm = TM_MLP
    n_ctx_tiles = rows.n_ctx // tm
    return pl.pallas_call(
        _final_norm_kernel,
        grid=(rows.n_lat // tm,),
        in_specs=[pl.BlockSpec((tm, d), lambda i: (i + n_ctx_tiles, 0)),
                  pl.BlockSpec((1, d), lambda i: (0, 0))],
        out_specs=pl.BlockSpec((tm, d), lambda i: (i, 0)),
        out_shape=jax.ShapeDtypeStruct((rows.n_lat, d), F32),
        compiler_params=_cparams(("parallel",)),
        name="final_norm",
    )(x2, norm_f.reshape(1, d))


def kernel(x, c, ctx, c_ctx, ada_w, ada_b, norm1, norm2, norm_f, mlp_w1, mlp_w2, hgrn_w_in, hgrn_lb,
           hgrn_gnorm, hgrn_w_out, conv_w_in, conv_w, conv_b, conv_w_out):
    bsz, seq, d = x.shape
    ctx_len = ctx.shape[1]
    depth = ada_w.shape[0]
    assert bsz <= CTX_MOD_ROW and d % HEAD_DIM == 0
    assert seq % max(TM_MLP, TM_CONV, T_SCAN) == 0 and ctx_len % T_SCAN == 0
    assert (bsz * ctx_len) % max(TM_MLP, TM_CONV) == 0
    assert ctx_len & (ctx_len - 1) == 0 and TM_CONV % ctx_len == 0
    rows = _Rows(bsz, seq, ctx_len, d)

    c8 = jnp.concatenate([c, c_ctx[None, :], jnp.zeros((MOD_ROWS - bsz - 1, d), F32)], axis=0)
    mods = _ada_table(c8, ada_w, ada_b).reshape(depth * MOD_ROWS * N_MODS, 1, d)
    lb = _lower_bounds(hgrn_lb).reshape(2, -1, 1, d)

    norm1 = norm1.reshape(depth, 1, d)
    norm2 = norm2.reshape(depth, 1, d)
    gnorm = hgrn_gnorm.reshape(-1, 1, d)
    conv_b = conv_b.reshape(-1, 1, d)
    w1, w2 = mlp_w1.astype(BF16), mlp_w2.astype(BF16)
    hw_in, hw_out = hgrn_w_in.astype(BF16), hgrn_w_out.astype(BF16)
    cw_in, cw_out = conv_w_in.astype(BF16), conv_w_out.astype(BF16)

    x2 = jnp.concatenate([ctx.reshape(bsz * ctx_len, d), x.reshape(bsz * seq, d)], axis=0)
    for i in range(depth):
        j = i // N_MIXERS
        if i % N_MIXERS == 0:
            qsf, ksf, qsb, ksb, v, gate, dec = _hgrn_prep(rows, x2, norm1, mods, hw_in, lb, i, j)
            o_bwd = _hgrn_scan_bwd(rows, qsb, ksb, v, dec)
            x2 = _hgrn_scan_fwd(rows, qsf, ksf, v, dec, gate, o_bwd, x2, gnorm, mods, hw_out, i, j)
        else:
            x2 = _conv_mixer(rows, x2, norm1, mods, cw_in, conv_w, conv_b, cw_out, i, j,
                             along_rows=(j % 2 == 1))
        x2 = _mlp(rows, x2, norm2, mods, w1, w2, i)
    return _final_norm(rows, x2, norm_f).reshape(bsz, seq, d)
```

```python
import functools

import jax
import jax.numpy as jnp
from jax import lax
from jax.experimental import pallas as pl
from jax.experimental.pallas import tpu as pltpu

F32 = jnp.float32
BF16 = jnp.bfloat16

GRID_W = 64
HEAD_DIM = 128
CHUNK = 64
EPS = 1e-6
N_MIXERS = 2
CTX_MOD_ROW = 4
MOD_ROWS = 8
N_MODS = 6
DEC_ROWS = 8
STRIP = 1024

VMEM_LIMIT = 56 * 1024 * 1024

TM_PREP = 512
TM_MLP = 512
TM_CONV = 512
T_SCAN = 256
TN_ADA = 2048


def _cparams(sem):
    return pltpu.CompilerParams(dimension_semantics=sem, vmem_limit_bytes=VMEM_LIMIT)


def _const_spec(shape, index_map):
    return pl.BlockSpec(shape, index_map, pipeline_mode=pl.Buffered(1))


def _silu(t):
    return t * jax.nn.sigmoid(t)


def _rms_mod(x, gain, shift, scale):
    y = x * lax.rsqrt(jnp.mean(x * x, axis=-1, keepdims=True) + EPS) * gain
    return y * (1.0 + scale) + shift


def _ada_kernel(c_ref, w_ref, b_ref, o_ref):
    s = _silu(c_ref[...]).astype(BF16)
    o_ref[...] = jnp.dot(s, w_ref[...].astype(BF16), preferred_element_type=F32) + b_ref[...]


def _ada_table(c8, ada_w, ada_b):
    depth, d, n = ada_w.shape
    return pl.pallas_call(
        _ada_kernel,
        grid=(depth, n // TN_ADA),
        in_specs=[
            pl.BlockSpec((MOD_ROWS, d), lambda l, j: (0, 0)),
            pl.BlockSpec((None, d, TN_ADA), lambda l, j: (l, 0, j)),
            pl.BlockSpec((None, 1, TN_ADA), lambda l, j: (l, 0, j)),
        ],
        out_specs=pl.BlockSpec((None, MOD_ROWS, TN_ADA), lambda l, j: (l, 0, j)),
        out_shape=jax.ShapeDtypeStruct((depth, MOD_ROWS, n), F32),
        compiler_params=_cparams(("parallel", "parallel")),
        name="ada_table",
    )(c8, ada_w, ada_b.reshape(depth, 1, n))


def _lb_kernel(x_ref, o_ref):
    n_rec = x_ref.shape[1]
    for d in range(x_ref.shape[0]):
        rows = [x_ref[d, j:j + 1, :] for j in range(n_rec)]
        m = functools.reduce(jnp.maximum, rows)
        e = [jnp.exp(r - m) for r in rows]
        tot = functools.reduce(lambda a, b: a + b, e)
        acc = None
        for j in range(n_rec):
            p = e[j] / tot
            acc = p if acc is None else acc + p
            o_ref[d, j:j + 1, :] = acc - e[0] / tot


def _lower_bounds(hgrn_lb):
    return pl.pallas_call(
        _lb_kernel,
        out_shape=jax.ShapeDtypeStruct(hgrn_lb.shape, F32),
        name="hgrn_lower_bounds",
    )(hgrn_lb)


class _Rows:
    def __init__(self, bsz, seq, ctx_len, d):
        self.bsz, self.seq, self.ctx_len, self.d = bsz, seq, ctx_len, d
        self.n_ctx = bsz * ctx_len
        self.n_lat = bsz * seq
        self.n_rows = self.n_ctx + self.n_lat

    def mod_row(self, tm):
        n_ctx_tiles = self.n_ctx // tm
        per_batch = self.seq // tm
        return lambda i: jnp.where(i < n_ctx_tiles, CTX_MOD_ROW, (i - n_ctx_tiles) // per_batch)


def _mod_block(d, layer, k, row_of_grid):
    return pl.BlockSpec((None, 1, d),
                        lambda *g: ((layer * MOD_ROWS + row_of_grid(*g)) * N_MODS + k, 0, 0))


def _layer_vec_spec(d, layer):
    return pl.BlockSpec((None, 1, d), lambda *g: (layer, 0, 0))


def _cumsum_rows(g, reverse):
    n = g.shape[0]
    row = lax.broadcasted_iota(jnp.int32, g.shape, 0)
    s = 1
    while s < n:
        if reverse:
            g = g + jnp.where(row < n - s, pltpu.roll(g, n - s, axis=0), 0.0)
        else:
            g = g + jnp.where(row >= s, pltpu.roll(g, s, axis=0), 0.0)
        s *= 2
    return g


def _split_sources(n_src, refs):
    return refs[:n_src], refs[n_src:]


def _load_rows(x_refs, is_ctx):
    if len(x_refs) == 1:
        return x_refs[0][...]
    return jnp.where(is_ctx, x_refs[0][...], x_refs[1][...])


def _prep_kernel(n_src, ctx_tiles, *refs):
    x_refs, refs = _split_sources(n_src, refs)
    (gain_ref, sh_ref, sc_ref, w_ref, lbf_ref, lbb_ref,
     qsf_ref, ksf_ref, qsb_ref, ksb_ref, v_ref, gate_ref, dec_ref, z_ref, h_ref) = refs
    tm, d = v_ref.shape
    step = pl.program_id(0)
    wr = step % 2
    rd = 1 - wr

    @pl.when(step == 0)
    def _():
        z_ref[1] = jnp.zeros(z_ref.shape[1:], F32)

    n_chunks = tm // CHUNK
    x = _load_rows(x_refs, step < ctx_tiles)
    h_ref[...] = _rms_mod(x, gain_ref[...], sh_ref[...], sc_ref[...]).astype(BF16)
    wp = d // 2
    n_pieces = w_ref.shape[1] // wp

    def project(piece):
        cols = slice(piece * wp, (piece + 1) * wp)
        res = jnp.dot(h_ref[...], w_ref[:, cols], preferred_element_type=F32)
        p, off = divmod(piece * wp, d)
        dst = slice(off, off + wp)
        if p == 2:
            v_ref[:, dst] = res.astype(BF16)
        elif p == 4:
            gate_ref[:, dst] = _silu(res).astype(BF16)
        else:
            zc = {0: 0, 1: d, 3: 2 * d}[p] + off
            z_ref[wr, :, zc:zc + wp] = res

    dec_ref[:, 4:, :] = jnp.zeros((n_chunks, DEC_ROWS - 4, d), F32)
    for ci in range(n_chunks):
        for piece in range(ci * n_pieces // n_chunks, (ci + 1) * n_pieces // n_chunks):
            project(piece)
        rows = slice(ci * CHUNK, (ci + 1) * CHUNK)
        for lo in range(0, d, STRIP):
            cols = slice(lo, lo + STRIP)
            q = _silu(z_ref[rd, rows, 2 * d + lo:2 * d + lo + STRIP])
            for dirn, (lb_ref, qs_ref, ks_ref) in enumerate(((lbf_ref, qsf_ref, ksf_ref),
                                                             (lbb_ref, qsb_ref, ksb_ref))):
                lb = lb_ref[:, cols]
                f = lb + (1.0 - lb) * jax.nn.sigmoid(z_ref[rd, rows, dirn * d + lo:dirn * d + lo + STRIP])
                cum = _cumsum_rows(jnp.log(f), dirn == 1)
                if dirn == 1:
                    ref, last = cum[CHUNK // 2:CHUNK // 2 + 1], cum[0:1]
                else:
                    ref, last = cum[CHUNK // 2 - 1:CHUNK // 2], cum[CHUNK - 1:CHUNK]
                qs_ref[rows, cols] = (q * jnp.exp(cum - ref)).astype(BF16)
                ks_ref[rows, cols] = ((1.0 - f) * jnp.exp(ref - cum)).astype(BF16)
                dec_ref[ci, 2 * dirn:2 * dirn + 1, cols] = jnp.exp(ref)
                dec_ref[ci, 2 * dirn + 1:2 * dirn + 2, cols] = jnp.exp(last - ref)


def _source_specs(rows, xs, tm, tile_of_grid):
    d = rows.d
    if len(xs) == 1:
        return [pl.BlockSpec((tm, d), lambda *g: (tile_of_grid(*g), 0))]
    n_ctx_tiles = rows.n_ctx // tm
    return [pl.BlockSpec((tm, d), lambda *g: (jnp.minimum(tile_of_grid(*g), n_ctx_tiles - 1), 0)),
            pl.BlockSpec((tm, d), lambda *g: (jnp.maximum(tile_of_grid(*g) - n_ctx_tiles, 0), 0))]


def _hgrn_prep(rows, xs, norm1, mods, w_in, lb, layer, j):
    d = rows.d
    n = w_in.shape[-1]
    tm = TM_PREP
    n_tiles = rows.n_rows // tm
    cur = lambda s: jnp.minimum(s, n_tiles - 1)
    prev = lambda s: jnp.maximum(s - 1, 0)
    row = lambda s: rows.mod_row(tm)(cur(s))
    act_cur = pl.BlockSpec((tm, d), lambda s: (cur(s), 0))
    act_prev = pl.BlockSpec((tm, d), lambda s: (prev(s), 0))
    act_shape = jax.ShapeDtypeStruct((rows.n_rows, d), BF16)
    return pl.pallas_call(
        functools.partial(_prep_kernel, len(xs), rows.n_ctx // tm),
        grid=(n_tiles + 1,),
        in_specs=_source_specs(rows, xs, tm, cur) + [
            _layer_vec_spec(d, layer),
            _mod_block(d, layer, 0, row),
            _mod_block(d, layer, 1, row),
            _const_spec((None, d, n), lambda s: (j, 0, 0)),
            pl.BlockSpec((None, None, 1, d), lambda s: (0, j, 0, 0)),
            pl.BlockSpec((None, None, 1, d), lambda s: (1, j, 0, 0)),
        ],
        out_specs=[act_prev] * 4 + [act_cur] * 2
        + [pl.BlockSpec((tm // CHUNK, DEC_ROWS, d), lambda s: (prev(s), 0, 0))],
        out_shape=[act_shape] * 6 + [jax.ShapeDtypeStruct((rows.n_rows // CHUNK, DEC_ROWS, d), F32)],
        scratch_shapes=[pltpu.VMEM((2, tm, 3 * d), F32), pltpu.VMEM((tm, d), BF16)],
        compiler_params=_cparams(("arbitrary",)),
        name="hgrn_prep",
    )(*xs, norm1, mods, mods, w_in, lb, lb)


def _gla_tile(qs_ref, ks_ref, v_ref, dec_ref, s_ref, dirn, emit):
    t_rows, d = qs_ref.shape
    reverse = dirn == 1
    n_chunks = t_rows // CHUNK
    r = lax.broadcasted_iota(jnp.int32, (CHUNK, CHUNK), 0)
    c = lax.broadcasted_iota(jnp.int32, (CHUNK, CHUNK), 1)
    mask = (c >= r) if reverse else (c <= r)
    nt = (((1,), (1,)), ((), ()))
    tn = (((0,), (0,)), ((), ()))
    order = range(n_chunks - 1, -1, -1) if reverse else range(n_chunks)
    heads = range(d // HEAD_DIM)
    hsl = lambda h: slice(h * HEAD_DIM, (h + 1) * HEAD_DIM)
    rsl = lambda ci: slice(ci * CHUNK, (ci + 1) * CHUNK)
    p, u = {}, {}
    for h in heads:
        for ci in order:
            sc = lax.dot_general(qs_ref[rsl(ci), hsl(h)], ks_ref[rsl(ci), hsl(h)], nt,
                                 preferred_element_type=F32)
            p[h, ci] = jnp.where(mask, sc, 0.0).astype(BF16)
    for h in heads:
        for ci in order:
            u[h, ci] = lax.dot_general(v_ref[rsl(ci), hsl(h)], ks_ref[rsl(ci), hsl(h)], tn,
                                       preferred_element_type=F32)
    for h in heads:
        st = s_ref[h]
        for ci in order:
            e_ref = dec_ref[ci, 2 * dirn:2 * dirn + 1, hsl(h)]
            e_lr = dec_ref[ci, 2 * dirn + 1:2 * dirn + 2, hsl(h)]
            o = lax.dot_general(qs_ref[rsl(ci), hsl(h)], (st * e_ref).astype(BF16), nt,
                                preferred_element_type=F32)
            o = o + jnp.dot(p[h, ci], v_ref[rsl(ci), hsl(h)], preferred_element_type=F32)
            st = st * (e_ref * e_lr) + u[h, ci] * e_lr
            emit(rsl(ci), hsl(h), o)
        s_ref[h] = st


def _scan_bwd_kernel(qs_ref, ks_ref, v_ref, dec_ref, o_ref, s_ref):
    @pl.when(pl.program_id(1) == 0)
    def _():
        s_ref[...] = jnp.zeros_like(s_ref)

    def emit(rows, hs, o):
        o_ref[rows, hs] = o

    _gla_tile(qs_ref, ks_ref, v_ref, dec_ref, s_ref, 1, emit)


def _scan_fwd_kernel(n_src, ctx_steps, *refs):
    x_refs, refs = _split_sources(n_src, refs)
    (qs_ref, ks_ref, v_ref, dec_ref, gate_ref, ob_ref, gn_ref, g1_ref, wout_ref,
     o_ref, s_ref, y_ref) = refs

    @pl.when(pl.program_id(1) == 0)
    def _():
        s_ref[...] = jnp.zeros_like(s_ref)

    def emit(rows, hs, o):
        o = o + ob_ref[rows, hs]
        o = o * lax.rsqrt(jnp.mean(o * o, axis=-1, keepdims=True) + EPS)
        y_ref[rows, hs] = (o * gn_ref[:, hs] * gate_ref[rows, hs].astype(F32)).astype(BF16)

    _gla_tile(qs_ref, ks_ref, v_ref, dec_ref, s_ref, 0, emit)
    y = jnp.dot(y_ref[...], wout_ref[...], preferred_element_type=F32)
    o_ref[...] = _load_rows(x_refs, pl.program_id(1) < ctx_steps) + g1_ref[...] * y


def _scan_tiles(rows, reverse):
    t = T_SCAN
    nct = rows.ctx_len // t
    ntl = rows.seq // t
    ctx_tiles = rows.n_ctx // t

    def tile(b, s):
        if reverse:
            return jnp.where(s < nct, b * nct + (nct - 1 - s), ctx_tiles + b * ntl + (ntl - 1 - (s - nct)))
        return jnp.where(s < nct, b * nct + s, ctx_tiles + b * ntl + (s - nct))

    mod_row = lambda b, s: jnp.where(s < nct, CTX_MOD_ROW, b)
    return tile, mod_row, nct + ntl


def _hgrn_scan_bwd(rows, qs, ks, v, dec):
    d = rows.d
    t = T_SCAN
    tile, _, steps = _scan_tiles(rows, True)
    act = pl.BlockSpec((t, d), lambda b, s: (tile(b, s), 0))
    return pl.pallas_call(
        _scan_bwd_kernel,
        grid=(rows.bsz, steps),
        in_specs=[act, act, act,
                  pl.BlockSpec((t // CHUNK, DEC_ROWS, d), lambda b, s: (tile(b, s), 0, 0))],
        out_specs=act,
        out_shape=jax.ShapeDtypeStruct((rows.n_rows, d), F32),
        scratch_shapes=[pltpu.VMEM((d // HEAD_DIM, HEAD_DIM, HEAD_DIM), F32)],
        compiler_params=_cparams(("arbitrary", "arbitrary")),
        name="hgrn_scan_bwd",
    )(qs, ks, v, dec)


def _hgrn_scan_fwd(rows, qs, ks, v, dec, gate, o_bwd, xs, gnorm, mods, w_out, layer, j):
    d = rows.d
    t = T_SCAN
    tile, mod_row, steps = _scan_tiles(rows, False)
    act = pl.BlockSpec((t, d), lambda b, s: (tile(b, s), 0))
    return pl.pallas_call(
        functools.partial(_scan_fwd_kernel, len(xs), rows.ctx_len // t),
        grid=(rows.bsz, steps),
        in_specs=_source_specs(rows, xs, t, tile) + [
                  act, act, act,
                  pl.BlockSpec((t // CHUNK, DEC_ROWS, d), lambda b, s: (tile(b, s), 0, 0)),
                  act, act,
                  _layer_vec_spec(d, j),
                  _mod_block(d, layer, 2, mod_row),
                  _const_spec((None, d, d), lambda b, s: (j, 0, 0))],
        out_specs=act,
        out_shape=jax.ShapeDtypeStruct((rows.n_rows, d), F32),
        scratch_shapes=[pltpu.VMEM((d // HEAD_DIM, HEAD_DIM, HEAD_DIM), F32),
                        pltpu.VMEM((t, d), BF16)],
        compiler_params=_cparams(("arbitrary", "arbitrary")),
        name="hgrn_scan_fwd",
    )(*xs, qs, ks, v, dec, gate, o_bwd, gnorm, mods, w_out)


def _conv_taps(cw_ref, cb_ref, um, u, up):
    return cb_ref[...] + cw_ref[0:1, :] * um + cw_ref[1:2, :] * u + cw_ref[2:3, :] * up


def _conv_seq_kernel(n_ctx_tiles, ctx_len, x_ref, gain_ref, sh_ref, sc_ref, g1_ref, win_ref,
                     cw_ref, cb_ref, wout_ref, o_ref):
    x = x_ref[...]
    tm, d = x.shape
    h = _rms_mod(x, gain_ref[...], sh_ref[...], sc_ref[...]).astype(BF16)
    p = jnp.dot(h, win_ref[...], preferred_element_type=F32)
    gate_b, u = p[:, :d], p[:, d:2 * d] * p[:, 2 * d:]
    period = jnp.where(pl.program_id(0) < n_ctx_tiles, ctx_len, GRID_W)
    pos = lax.broadcasted_iota(jnp.int32, (tm, 1), 0) & (period - 1)
    um = jnp.where(pos == 0, 0.0, pltpu.roll(u, 1, axis=0))
    up = jnp.where(pos == period - 1, 0.0, pltpu.roll(u, tm - 1, axis=0))
    cv = _conv_taps(cw_ref, cb_ref, um, u, up)
    y = jnp.dot((gate_b * cv).astype(BF16), wout_ref[...], preferred_element_type=F32)
    o_ref[...] = x + g1_ref[...] * y


def _conv_rows_kernel(n_ctx_tiles, tiles_per_batch, xp_ref, x_ref, xn_ref, gain_ref, sh_ref,
                      sc_ref, g1_ref, win_ref, cw_ref, cb_ref, wout_ref, o_ref, h_ref):
    x = x_ref[...]
    tm, d = x.shape
    gain, sh, sc = gain_ref[...], sh_ref[...], sc_ref[...]
    h_ref[0:GRID_W, :] = _rms_mod(xp_ref[...], gain, sh, sc).astype(BF16)
    h_ref[GRID_W:GRID_W + tm, :] = _rms_mod(x, gain, sh, sc).astype(BF16)
    h_ref[GRID_W + tm:, :] = _rms_mod(xn_ref[...], gain, sh, sc).astype(BF16)
    p = jnp.dot(h_ref[...], win_ref[:, d:], preferred_element_type=F32)
    u = p[:, :d] * p[:, d:]
    gate_b = jnp.dot(h_ref[GRID_W:GRID_W + tm, :], win_ref[:, :d], preferred_element_type=F32)
    i = pl.program_id(0)
    jl = (i - n_ctx_tiles) % tiles_per_batch
    is_lat = i >= n_ctx_tiles
    has_prev = jnp.logical_and(is_lat, jl != 0)
    has_next = jnp.logical_and(is_lat, jl != tiles_per_batch - 1)
    row = lax.broadcasted_iota(jnp.int32, (tm, 1), 0)
    um = jnp.where(jnp.logical_or(row >= GRID_W, has_prev), u[0:tm], 0.0)
    up = jnp.where(jnp.logical_or(row < tm - GRID_W, has_next), u[2 * GRID_W:], 0.0)
    cv = _conv_taps(cw_ref, cb_ref, um, u[GRID_W:GRID_W + tm], up)
    y = jnp.dot((gate_b * cv).astype(BF16), wout_ref[...], preferred_element_type=F32)
    o_ref[...] = x + g1_ref[...] * y


def _conv_mixer(rows, x2, norm1, mods, w_in, cw, cb, w_out, layer, j, along_rows):
    d = rows.d
    tm = TM_CONV
    row = rows.mod_row(tm)
    n_tiles = rows.n_rows // tm
    n_ctx_tiles = rows.n_ctx // tm
    common = [
        _layer_vec_spec(d, layer),
        _mod_block(d, layer, 0, row),
        _mod_block(d, layer, 1, row),
        _mod_block(d, layer, 2, row),
        _const_spec((None, d, 3 * d), lambda i: (j, 0, 0)),
        pl.BlockSpec((None, 3, d), lambda i: (j, 0, 0)),
        _layer_vec_spec(d, j),
        _const_spec((None, d, d), lambda i: (j, 0, 0)),
    ]
    args = (norm1, mods, mods, mods, w_in, cw, cb, w_out)
    if along_rows:
        sub = tm // GRID_W
        last = rows.n_rows // GRID_W - 1
        kern = functools.partial(_conv_rows_kernel, n_ctx_tiles, rows.seq // tm)
        in_specs = [pl.BlockSpec((GRID_W, d), lambda i: (jnp.maximum(i * sub - 1, 0), 0)),
                    pl.BlockSpec((tm, d), lambda i: (i, 0)),
                    pl.BlockSpec((GRID_W, d), lambda i: (jnp.minimum((i + 1) * sub, last), 0))]
        args = (x2, x2, x2) + args
        scratch = [pltpu.VMEM((tm + 2 * GRID_W, d), BF16)]
        name = "conv_mixer_rows"
    else:
        kern = functools.partial(_conv_seq_kernel, n_ctx_tiles, rows.ctx_len)
        in_specs = [pl.BlockSpec((tm, d), lambda i: (i, 0))]
        args = (x2,) + args
        scratch = []
        name = "conv_mixer_seq"
    return pl.pallas_call(
        kern,
        grid=(n_tiles,),
        in_specs=in_specs + common,
        out_specs=pl.BlockSpec((tm, d), lambda i: (i, 0)),
        out_shape=jax.ShapeDtypeStruct((rows.n_rows, d), F32),
        scratch_shapes=scratch,
        compiler_params=_cparams(("parallel",)),
        name=name,
    )(*args)


def _mlp_residual(x_ref, gain_ref, sh_ref, sc_ref, g2_ref, w1_ref, w2_ref):
    x = x_ref[...]
    h = _rms_mod(x, gain_ref[...], sh_ref[...], sc_ref[...]).astype(BF16)
    a = jnp.dot(h, w1_ref[...], preferred_element_type=F32)
    a = jnp.square(jnp.maximum(a, 0.0)).astype(BF16)
    y = jnp.dot(a, w2_ref[...], preferred_element_type=F32)
    return x + g2_ref[...] * y


def _mlp_kernel(x_ref, gain_ref, sh_ref, sc_ref, g2_ref, w1_ref, w2_ref, o_ref):
    o_ref[...] = _mlp_residual(x_ref, gain_ref, sh_ref, sc_ref, g2_ref, w1_ref, w2_ref)


def _mlp_final_kernel(x_ref, gain_ref, sh_ref, sc_ref, g2_ref, w1_ref, w2_ref, gf_ref, o_ref):
    x = _mlp_residual(x_ref, gain_ref, sh_ref, sc_ref, g2_ref, w1_ref, w2_ref)
    o_ref[...] = x * lax.rsqrt(jnp.mean(x * x, axis=-1, keepdims=True) + EPS) * gf_ref[...]


def _mlp(rows, x2, norm2, mods, w1, w2, layer, norm_f=None):
    d = rows.d
    d_ff = w1.shape[-1]
    tm = TM_MLP
    final = norm_f is not None
    first = rows.n_ctx // tm if final else 0
    n_out = rows.n_lat if final else rows.n_rows
    row = lambda i: rows.mod_row(tm)(i + first)
    in_specs = [
        pl.BlockSpec((tm, d), lambda i: (i + first, 0)),
        _layer_vec_spec(d, layer),
        _mod_block(d, layer, 3, row),
        _mod_block(d, layer, 4, row),
        _mod_block(d, layer, 5, row),
        _const_spec((None, d, d_ff), lambda i: (layer, 0, 0)),
        _const_spec((None, d_ff, d), lambda i: (layer, 0, 0)),
    ]
    args = (x2, norm2, mods, mods, mods, w1, w2)
    if final:
        in_specs.append(pl.BlockSpec((1, d), lambda i: (0, 0)))
        args += (norm_f.reshape(1, d),)
    return pl.pallas_call(
        _mlp_final_kernel if final else _mlp_kernel,
        grid=(n_out // tm,),
        in_specs=in_specs,
        out_specs=pl.BlockSpec((tm, d), lambda i: (i, 0)),
        out_shape=jax.ShapeDtypeStruct((n_out, d), F32),
        compiler_params=_cparams(("parallel",)),
        name="mlp_final" if final else "mlp",
    )(*args)


def kernel(x, c, ctx, c_ctx, ada_w, ada_b, norm1, norm2, norm_f, mlp_w1, mlp_w2, hgrn_w_in, hgrn_lb,
           hgrn_gnorm, hgrn_w_out, conv_w_in, conv_w, conv_b, conv_w_out):
    bsz, seq, d = x.shape
    ctx_len = ctx.shape[1]
    depth = ada_w.shape[0]
    assert bsz <= CTX_MOD_ROW and d % HEAD_DIM == 0
    assert seq % max(TM_MLP, TM_CONV, T_SCAN) == 0 and ctx_len % T_SCAN == 0
    assert (bsz * ctx_len) % max(TM_MLP, TM_CONV) == 0
    assert ctx_len & (ctx_len - 1) == 0 and TM_CONV % ctx_len == 0
    rows = _Rows(bsz, seq, ctx_len, d)

    c8 = jnp.concatenate([c, c_ctx[None, :], jnp.zeros((MOD_ROWS - bsz - 1, d), F32)], axis=0)
    mods = _ada_table(c8, ada_w, ada_b).reshape(depth * MOD_ROWS * N_MODS, 1, d)
    lb = _lower_bounds(hgrn_lb).reshape(2, -1, 1, d)

    norm1 = norm1.reshape(depth, 1, d)
    norm2 = norm2.reshape(depth, 1, d)
    gnorm = hgrn_gnorm.reshape(-1, 1, d)
    conv_b = conv_b.reshape(-1, 1, d)
    w1, w2 = mlp_w1.astype(BF16), mlp_w2.astype(BF16)
    hw_in, hw_out = hgrn_w_in.astype(BF16), hgrn_w_out.astype(BF16)
    cw_in, cw_out = conv_w_in.astype(BF16), conv_w_out.astype(BF16)

    xs = (ctx.reshape(bsz * ctx_len, d), x.reshape(bsz * seq, d))
    for i in range(depth):
        j = i // N_MIXERS
        if i % N_MIXERS == 0:
            qsf, ksf, qsb, ksb, v, gate, dec = _hgrn_prep(rows, xs, norm1, mods, hw_in, lb, i, j)
            o_bwd = _hgrn_scan_bwd(rows, qsb, ksb, v, dec)
            x2 = _hgrn_scan_fwd(rows, qsf, ksf, v, dec, gate, o_bwd, xs, gnorm, mods, hw_out, i, j)
        else:
            x2 = _conv_mixer(rows, x2, norm1, mods, cw_in, conv_w, conv_b, cw_out, i, j,
                             along_rows=(j % 2 == 1))
        x2 = _mlp(rows, x2, norm2, mods, w1, w2, i, norm_f if i == depth - 1 else None)
        xs = (x2,)
    return x2.reshape(bsz, seq, d)
```

```python
import functools

import jax
import jax.numpy as jnp
from jax import lax
from jax.experimental import pallas as pl
from jax.experimental.pallas import tpu as pltpu

F32 = jnp.float32
BF16 = jnp.bfloat16

GRID_W = 64
HEAD_DIM = 128
CHUNK = 64
EPS = 1e-6
N_MIXERS = 2
CTX_MOD_ROW = 4
MOD_ROWS = 8
N_MODS = 6
DEC_ROWS = 8

VMEM_LIMIT = 56 * 1024 * 1024

TM_PREP = 256
TM_MLP = 512
TM_CONV = 512
T_SCAN = 256
TN_ADA = 2048


def _cparams(sem):
    return pltpu.CompilerParams(dimension_semantics=sem, vmem_limit_bytes=VMEM_LIMIT)


def _const_spec(shape, index_map):
    return pl.BlockSpec(shape, index_map, pipeline_mode=pl.Buffered(1))


def _silu(t):
    return t * jax.nn.sigmoid(t)


def _rms_mod(x, gain, shift, scale):
    y = x * lax.rsqrt(jnp.mean(x * x, axis=-1, keepdims=True) + EPS) * gain
    return y * (1.0 + scale) + shift


def _ada_kernel(c_ref, w_ref, b_ref, o_ref):
    s = _silu(c_ref[...]).astype(BF16)
    o_ref[...] = jnp.dot(s, w_ref[...].astype(BF16), preferred_element_type=F32) + b_ref[...]


def _ada_table(c8, ada_w, ada_b):
    depth, d, n = ada_w.shape
    return pl.pallas_call(
        _ada_kernel,
        grid=(depth, n // TN_ADA),
        in_specs=[
            pl.BlockSpec((MOD_ROWS, d), lambda l, j: (0, 0)),
            pl.BlockSpec((None, d, TN_ADA), lambda l, j: (l, 0, j)),
            pl.BlockSpec((None, 1, TN_ADA), lambda l, j: (l, 0, j)),
        ],
        out_specs=pl.BlockSpec((None, MOD_ROWS, TN_ADA), lambda l, j: (l, 0, j)),
        out_shape=jax.ShapeDtypeStruct((depth, MOD_ROWS, n), F32),
        compiler_params=_cparams(("parallel", "parallel")),
        name="ada_table",
    )(c8, ada_w, ada_b.reshape(depth, 1, n))


def _lb_kernel(x_ref, o_ref):
    n_rec = x_ref.shape[1]
    for d in range(x_ref.shape[0]):
        rows = [x_ref[d, j:j + 1, :] for j in range(n_rec)]
        m = functools.reduce(jnp.maximum, rows)
        e = [jnp.exp(r - m) for r in rows]
        tot = functools.reduce(lambda a, b: a + b, e)
        acc = None
        for j in range(n_rec):
            p = e[j] / tot
            acc = p if acc is None else acc + p
            o_ref[d, j:j + 1, :] = acc - e[0] / tot


def _lower_bounds(hgrn_lb):
    return pl.pallas_call(
        _lb_kernel,
        out_shape=jax.ShapeDtypeStruct(hgrn_lb.shape, F32),
        name="hgrn_lower_bounds",
    )(hgrn_lb)


class _Rows:
    def __init__(self, bsz, seq, ctx_len, d):
        self.bsz, self.seq, self.ctx_len, self.d = bsz, seq, ctx_len, d
        self.n_ctx = bsz * ctx_len
        self.n_lat = bsz * seq
        self.n_rows = self.n_ctx + self.n_lat

    def mod_row(self, tm):
        n_ctx_tiles = self.n_ctx // tm
        per_batch = self.seq // tm
        return lambda i: jnp.where(i < n_ctx_tiles, CTX_MOD_ROW, (i - n_ctx_tiles) // per_batch)


def _mod_block(d, layer, k, row_of_grid):
    return pl.BlockSpec((None, 1, d),
                        lambda *g: ((layer * MOD_ROWS + row_of_grid(*g)) * N_MODS + k, 0, 0))


def _layer_vec_spec(d, layer):
    return pl.BlockSpec((None, 1, d), lambda *g: (layer, 0, 0))


def _cumsum_rows(g, reverse):
    n = g.shape[0]
    row = lax.broadcasted_iota(jnp.int32, g.shape, 0)
    s = 1
    while s < n:
        if reverse:
            g = g + jnp.where(row < n - s, pltpu.roll(g, n - s, axis=0), 0.0)
        else:
            g = g + jnp.where(row >= s, pltpu.roll(g, s, axis=0), 0.0)
        s *= 2
    return g


def _split_sources(n_src, refs):
    return refs[:n_src], refs[n_src:]


def _load_rows(x_refs, is_ctx):
    if len(x_refs) == 1:
        return x_refs[0][...]
    return jnp.where(is_ctx, x_refs[0][...], x_refs[1][...])


def _prep_kernel(n_src, ctx_tiles, *refs):
    x_refs, refs = _split_sources(n_src, refs)
    (gain_ref, sh_ref, sc_ref, w_ref, lbf_ref, lbb_ref,
     qsf_ref, ksf_ref, qsb_ref, ksb_ref, v_ref, gate_ref, dec_ref, za_ref, zb_ref, h_ref) = refs
    tm, d = v_ref.shape
    n_chunks = tm // CHUNK
    ws = d // n_chunks
    n_parts = w_ref.shape[-1] // ws
    step = pl.program_id(0)

    @pl.when(step == 0)
    def _():
        zb_ref[...] = jnp.zeros(zb_ref.shape, F32)

    x = _load_rows(x_refs, step < ctx_tiles)
    h_ref[...] = _rms_mod(x, gain_ref[...], sh_ref[...], sc_ref[...]).astype(BF16)
    dec_ref[:, 4:, :] = jnp.zeros((n_chunks, DEC_ROWS - 4, d), F32)
    outs = ((lbf_ref, qsf_ref, ksf_ref), (lbb_ref, qsb_ref, ksb_ref))
    assert n_parts == 5

    def trip(zw_ref, zr_ref, it, carry):
        zw_ref[it] = jnp.dot(h_ref[...], w_ref[it], preferred_element_type=F32)
        rows = pl.ds(pl.multiple_of(it * CHUNK, CHUNK), CHUNK)
        for j in range(n_chunks):
            cols = slice(j * ws, (j + 1) * ws)
            part = lambda p: zr_ref[j, rows, p * ws:(p + 1) * ws]
            v_ref[rows, cols] = part(2).astype(BF16)
            gate_ref[rows, cols] = _silu(part(4)).astype(BF16)
            q = _silu(part(3))
            for dirn, (lb_ref, qs_ref, ks_ref) in enumerate(outs):
                lb = lb_ref[:, cols]
                f = lb + (1.0 - lb) * jax.nn.sigmoid(part(dirn))
                cum = _cumsum_rows(jnp.log(f), dirn == 1)
                if dirn == 1:
                    ref, last = cum[CHUNK // 2:CHUNK // 2 + 1], cum[0:1]
                else:
                    ref, last = cum[CHUNK // 2 - 1:CHUNK // 2], cum[CHUNK - 1:CHUNK]
                qs_ref[rows, cols] = (q * jnp.exp(cum - ref)).astype(BF16)
                ks_ref[rows, cols] = ((1.0 - f) * jnp.exp(ref - cum)).astype(BF16)
                dec_ref[it, 2 * dirn:2 * dirn + 1, cols] = jnp.exp(ref)
                dec_ref[it, 2 * dirn + 1:2 * dirn + 2, cols] = jnp.exp(last - ref)
        return carry

    @pl.when(step % 2 == 0)
    def _():
        lax.fori_loop(0, n_chunks, functools.partial(trip, za_ref, zb_ref), 0)

    @pl.when(step % 2 == 1)
    def _():
        lax.fori_loop(0, n_chunks, functools.partial(trip, zb_ref, za_ref), 0)


def _source_specs(rows, xs, tm, tile_of_grid):
    d = rows.d
    if len(xs) == 1:
        return [pl.BlockSpec((tm, d), lambda *g: (tile_of_grid(*g), 0))]
    n_ctx_tiles = rows.n_ctx // tm
    return [pl.BlockSpec((tm, d), lambda *g: (jnp.minimum(tile_of_grid(*g), n_ctx_tiles - 1), 0)),
            pl.BlockSpec((tm, d), lambda *g: (jnp.maximum(tile_of_grid(*g) - n_ctx_tiles, 0), 0))]


def _prep_weight_pieces(w_in, d):
    n_layers, _, n_cols = w_in.shape
    n = TM_PREP // CHUNK
    w = w_in.reshape(n_layers, d, n_cols // d, n, d // n)
    return w.transpose(0, 3, 1, 2, 4).reshape(n_layers, n, d, n_cols // n).astype(BF16)


def _hgrn_prep(rows, xs, norm1, mods, w_pieces, lb, layer, j):
    d = rows.d
    tm = TM_PREP
    n_pieces, _, piece_cols = w_pieces.shape[1:]
    n_tiles = rows.n_rows // tm
    cur = lambda s: jnp.minimum(s, n_tiles - 1)
    prev = lambda s: jnp.maximum(s - 1, 0)
    row = lambda s: rows.mod_row(tm)(cur(s))
    act_prev = pl.BlockSpec((tm, d), lambda s: (prev(s), 0))
    act_shape = jax.ShapeDtypeStruct((rows.n_rows, d), BF16)
    return pl.pallas_call(
        functools.partial(_prep_kernel, len(xs), rows.n_ctx // tm),
        grid=(n_tiles + 1,),
        in_specs=_source_specs(rows, xs, tm, cur) + [
            _layer_vec_spec(d, layer),
            _mod_block(d, layer, 0, row),
            _mod_block(d, layer, 1, row),
            _const_spec((None, n_pieces, d, piece_cols), lambda s: (j, 0, 0, 0)),
            pl.BlockSpec((None, None, 1, d), lambda s: (0, j, 0, 0)),
            pl.BlockSpec((None, None, 1, d), lambda s: (1, j, 0, 0)),
        ],
        out_specs=[act_prev] * 6
        + [pl.BlockSpec((tm // CHUNK, DEC_ROWS, d), lambda s: (prev(s), 0, 0))],
        out_shape=[act_shape] * 6 + [jax.ShapeDtypeStruct((rows.n_rows // CHUNK, DEC_ROWS, d), F32)],
        scratch_shapes=[pltpu.VMEM((n_pieces, tm, piece_cols), F32),
                        pltpu.VMEM((n_pieces, tm, piece_cols), F32),
                        pltpu.VMEM((tm, d), BF16)],
        compiler_params=_cparams(("arbitrary",)),
        name="hgrn_prep",
    )(*xs, norm1, mods, mods, w_pieces, lb, lb)


def _gla_tile(qs_ref, ks_ref, v_ref, dec_ref, s_ref, dirn, emit):
    t_rows, d = qs_ref.shape
    reverse = dirn == 1
    n_chunks = t_rows // CHUNK
    r = lax.broadcasted_iota(jnp.int32, (CHUNK, CHUNK), 0)
    c = lax.broadcasted_iota(jnp.int32, (CHUNK, CHUNK), 1)
    mask = (c >= r) if reverse else (c <= r)
    nt = (((1,), (1,)), ((), ()))
    tn = (((0,), (0,)), ((), ()))
    order = range(n_chunks - 1, -1, -1) if reverse else range(n_chunks)
    heads = range(d // HEAD_DIM)
    hsl = lambda h: slice(h * HEAD_DIM, (h + 1) * HEAD_DIM)
    rsl = lambda ci: slice(ci * CHUNK, (ci + 1) * CHUNK)
    p, u = {}, {}
    for h in heads:
        for ci in order:
            sc = lax.dot_general(qs_ref[rsl(ci), hsl(h)], ks_ref[rsl(ci), hsl(h)], nt,
                                 preferred_element_type=F32)
            p[h, ci] = jnp.where(mask, sc, 0.0).astype(BF16)
    for h in heads:
        for ci in order:
            u[h, ci] = lax.dot_general(v_ref[rsl(ci), hsl(h)], ks_ref[rsl(ci), hsl(h)], tn,
                                       preferred_element_type=F32)
    for h in heads:
        st = s_ref[h]
        for ci in order:
            e_ref = dec_ref[ci, 2 * dirn:2 * dirn + 1, hsl(h)]
            e_lr = dec_ref[ci, 2 * dirn + 1:2 * dirn + 2, hsl(h)]
            o = lax.dot_general(qs_ref[rsl(ci), hsl(h)], (st * e_ref).astype(BF16), nt,
                                preferred_element_type=F32)
            o = o + jnp.dot(p[h, ci], v_ref[rsl(ci), hsl(h)], preferred_element_type=F32)
            st = st * (e_ref * e_lr) + u[h, ci] * e_lr
            emit(rsl(ci), hsl(h), o)
        s_ref[h] = st


def _scan_bwd_kernel(qs_ref, ks_ref, v_ref, dec_ref, o_ref, s_ref):
    @pl.when(pl.program_id(1) == 0)
    def _():
        s_ref[...] = jnp.zeros_like(s_ref)

    def emit(rows, hs, o):
        o_ref[rows, hs] = o

    _gla_tile(qs_ref, ks_ref, v_ref, dec_ref, s_ref, 1, emit)


def _scan_fwd_kernel(n_src, ctx_steps, *refs):
    x_refs, refs = _split_sources(n_src, refs)
    (qs_ref, ks_ref, v_ref, dec_ref, gate_ref, ob_ref, gn_ref, g1_ref, wout_ref,
     o_ref, s_ref, y_ref) = refs

    @pl.when(pl.program_id(1) == 0)
    def _():
        s_ref[...] = jnp.zeros_like(s_ref)

    def emit(rows, hs, o):
        o = o + ob_ref[rows, hs]
        o = o * lax.rsqrt(jnp.mean(o * o, axis=-1, keepdims=True) + EPS)
        y_ref[rows, hs] = (o * gn_ref[:, hs] * gate_ref[rows, hs].astype(F32)).astype(BF16)

    _gla_tile(qs_ref, ks_ref, v_ref, dec_ref, s_ref, 0, emit)
    y = jnp.dot(y_ref[...], wout_ref[...], preferred_element_type=F32)
    o_ref[...] = _load_rows(x_refs, pl.program_id(1) < ctx_steps) + g1_ref[...] * y


def _scan_tiles(rows, reverse):
    t = T_SCAN
    nct = rows.ctx_len // t
    ntl = rows.seq // t
    ctx_tiles = rows.n_ctx // t

    def tile(b, s):
        if reverse:
            return jnp.where(s < nct, b * nct + (nct - 1 - s), ctx_tiles + b * ntl + (ntl - 1 - (s - nct)))
        return jnp.where(s < nct, b * nct + s, ctx_tiles + b * ntl + (s - nct))

    mod_row = lambda b, s: jnp.where(s < nct, CTX_MOD_ROW, b)
    return tile, mod_row, nct + ntl


def _hgrn_scan_bwd(rows, qs, ks, v, dec):
    d = rows.d
    t = T_SCAN
    tile, _, steps = _scan_tiles(rows, True)
    act = pl.BlockSpec((t, d), lambda b, s: (tile(b, s), 0))
    return pl.pallas_call(
        _scan_bwd_kernel,
        grid=(rows.bsz, steps),
        in_specs=[act, act, act,
                  pl.BlockSpec((t // CHUNK, DEC_ROWS, d), lambda b, s: (tile(b, s), 0, 0))],
        out_specs=act,
        out_shape=jax.ShapeDtypeStruct((rows.n_rows, d), F32),
        scratch_shapes=[pltpu.VMEM((d // HEAD_DIM, HEAD_DIM, HEAD_DIM), F32)],
        compiler_params=_cparams(("arbitrary", "arbitrary")),
        name="hgrn_scan_bwd",
    )(qs, ks, v, dec)


def _hgrn_scan_fwd(rows, qs, ks, v, dec, gate, o_bwd, xs, gnorm, mods, w_out, layer, j):
    d = rows.d
    t = T_SCAN
    tile, mod_row, steps = _scan_tiles(rows, False)
    act = pl.BlockSpec((t, d), lambda b, s: (tile(b, s), 0))
    return pl.pallas_call(
        functools.partial(_scan_fwd_kernel, len(xs), rows.ctx_len // t),
        grid=(rows.bsz, steps),
        in_specs=_source_specs(rows, xs, t, tile) + [
                  act, act, act,
                  pl.BlockSpec((t // CHUNK, DEC_ROWS, d), lambda b, s: (tile(b, s), 0, 0)),
                  act, act,
                  _layer_vec_spec(d, j),
                  _mod_block(d, layer, 2, mod_row),
                  _const_spec((None, d, d), lambda b, s: (j, 0, 0))],
        out_specs=act,
        out_shape=jax.ShapeDtypeStruct((rows.n_rows, d), F32),
        scratch_shapes=[pltpu.VMEM((d // HEAD_DIM, HEAD_DIM, HEAD_DIM), F32),
                        pltpu.VMEM((t, d), BF16)],
        compiler_params=_cparams(("arbitrary", "arbitrary")),
        name="hgrn_scan_fwd",
    )(*xs, qs, ks, v, dec, gate, o_bwd, gnorm, mods, w_out)


def _conv_taps(cw_ref, cb_ref, um, u, up):
    return cb_ref[...] + cw_ref[0:1, :] * um + cw_ref[1:2, :] * u + cw_ref[2:3, :] * up


def _conv_seq_kernel(n_ctx_tiles, ctx_len, x_ref, gain_ref, sh_ref, sc_ref, g1_ref, win_ref,
                     cw_ref, cb_ref, wout_ref, o_ref):
    x = x_ref[...]
    tm, d = x.shape
    h = _rms_mod(x, gain_ref[...], sh_ref[...], sc_ref[...]).astype(BF16)
    p = jnp.dot(h, win_ref[...], preferred_element_type=F32)
    gate_b, u = p[:, :d], p[:, d:2 * d] * p[:, 2 * d:]
    period = jnp.where(pl.program_id(0) < n_ctx_tiles, ctx_len, GRID_W)
    pos = lax.broadcasted_iota(jnp.int32, (tm, 1), 0) & (period - 1)
    um = jnp.where(pos == 0, 0.0, pltpu.roll(u, 1, axis=0))
    up = jnp.where(pos == period - 1, 0.0, pltpu.roll(u, tm - 1, axis=0))
    cv = _conv_taps(cw_ref, cb_ref, um, u, up)
    y = jnp.dot((gate_b * cv).astype(BF16), wout_ref[...], preferred_element_type=F32)
    o_ref[...] = x + g1_ref[...] * y


def _conv_rows_kernel(n_ctx_tiles, tiles_per_batch, xp_ref, x_ref, xn_ref, gain_ref, sh_ref,
                      sc_ref, g1_ref, win_ref, cw_ref, cb_ref, wout_ref, o_ref, h_ref):
    x = x_ref[...]
    tm, d = x.shape
    gain, sh, sc = gain_ref[...], sh_ref[...], sc_ref[...]
    h_ref[0:GRID_W, :] = _rms_mod(xp_ref[...], gain, sh, sc).astype(BF16)
    h_ref[GRID_W:GRID_W + tm, :] = _rms_mod(x, gain, sh, sc).astype(BF16)
    h_ref[GRID_W + tm:, :] = _rms_mod(xn_ref[...], gain, sh, sc).astype(BF16)
    p = jnp.dot(h_ref[...], win_ref[:, d:], preferred_element_type=F32)
    u = p[:, :d] * p[:, d:]
    gate_b = jnp.dot(h_ref[GRID_W:GRID_W + tm, :], win_ref[:, :d], preferred_element_type=F32)
    i = pl.program_id(0)
    jl = (i - n_ctx_tiles) % tiles_per_batch
    is_lat = i >= n_ctx_tiles
    has_prev = jnp.logical_and(is_lat, jl != 0)
    has_next = jnp.logical_and(is_lat, jl != tiles_per_batch - 1)
    row = lax.broadcasted_iota(jnp.int32, (tm, 1), 0)
    um = jnp.where(jnp.logical_or(row >= GRID_W, has_prev), u[0:tm], 0.0)
    up = jnp.where(jnp.logical_or(row < tm - GRID_W, has_next), u[2 * GRID_W:], 0.0)
    cv = _conv_taps(cw_ref, cb_ref, um, u[GRID_W:GRID_W + tm], up)
    y = jnp.dot((gate_b * cv).astype(BF16), wout_ref[...], preferred_element_type=F32)
    o_ref[...] = x + g1_ref[...] * y


def _conv_mixer(rows, x2, norm1, mods, w_in, cw, cb, w_out, layer, j, along_rows):
    d = rows.d
    tm = TM_CONV
    row = rows.mod_row(tm)
    n_tiles = rows.n_rows // tm
    n_ctx_tiles = rows.n_ctx // tm
    common = [
        _layer_vec_spec(d, layer),
        _mod_block(d, layer, 0, row),
        _mod_block(d, layer, 1, row),
        _mod_block(d, layer, 2, row),
        _const_spec((None, d, 3 * d), lambda i: (j, 0, 0)),
        pl.BlockSpec((None, 3, d), lambda i: (j, 0, 0)),
        _layer_vec_spec(d, j),
        _const_spec((None, d, d), lambda i: (j, 0, 0)),
    ]
    args = (norm1, mods, mods, mods, w_in, cw, cb, w_out)
    if along_rows:
        sub = tm // GRID_W
        last = rows.n_rows // GRID_W - 1
        kern = functools.partial(_conv_rows_kernel, n_ctx_tiles, rows.seq // tm)
        in_specs = [pl.BlockSpec((GRID_W, d), lambda i: (jnp.maximum(i * sub - 1, 0), 0)),
                    pl.BlockSpec((tm, d), lambda i: (i, 0)),
                    pl.BlockSpec((GRID_W, d), lambda i: (jnp.minimum((i + 1) * sub, last), 0))]
        args = (x2, x2, x2) + args
        scratch = [pltpu.VMEM((tm + 2 * GRID_W, d), BF16)]
        name = "conv_mixer_rows"
    else:
        kern = functools.partial(_conv_seq_kernel, n_ctx_tiles, rows.ctx_len)
        in_specs = [pl.BlockSpec((tm, d), lambda i: (i, 0))]
        args = (x2,) + args
        scratch = []
        name = "conv_mixer_seq"
    return pl.pallas_call(
        kern,
        grid=(n_tiles,),
        in_specs=in_specs + common,
        out_specs=pl.BlockSpec((tm, d), lambda i: (i, 0)),
        out_shape=jax.ShapeDtypeStruct((rows.n_rows, d), F32),
        scratch_shapes=scratch,
        compiler_params=_cparams(("parallel",)),
        name=name,
    )(*args)


def _mlp_residual(x_ref, gain_ref, sh_ref, sc_ref, g2_ref, w1_ref, w2_ref):
    x = x_ref[...]
    h = _rms_mod(x, gain_ref[...], sh_ref[...], sc_ref[...]).astype(BF16)
    a = jnp.dot(h, w1_ref[...], preferred_element_type=F32)
    a = jnp.square(jnp.maximum(a, 0.0)).astype(BF16)
    y = jnp.dot(a, w2_ref[...], preferred_element_type=F32)
    return x + g2_ref[...] * y


def _mlp_kernel(x_ref, gain_ref, sh_ref, sc_ref, g2_ref, w1_ref, w2_ref, o_ref):
    o_ref[...] = _mlp_residual(x_ref, gain_ref, sh_ref, sc_ref, g2_ref, w1_ref, w2_ref)


def _mlp_final_kernel(x_ref, gain_ref, sh_ref, sc_ref, g2_ref, w1_ref, w2_ref, gf_ref, o_ref):
    x = _mlp_residual(x_ref, gain_ref, sh_ref, sc_ref, g2_ref, w1_ref, w2_ref)
    o_ref[...] = x * lax.rsqrt(jnp.mean(x * x, axis=-1, keepdims=True) + EPS) * gf_ref[...]


def _mlp(rows, x2, norm2, mods, w1, w2, layer, norm_f=None):
    d = rows.d
    d_ff = w1.shape[-1]
    tm = TM_MLP
    final = norm_f is not None
    first = rows.n_ctx // tm if final else 0
    n_out = rows.n_lat if final else rows.n_rows
    row = lambda i: rows.mod_row(tm)(i + first)
    in_specs = [
        pl.BlockSpec((tm, d), lambda i: (i + first, 0)),
        _layer_vec_spec(d, layer),
        _mod_block(d, layer, 3, row),
        _mod_block(d, layer, 4, row),
        _mod_block(d, layer, 5, row),
        _const_spec((None, d, d_ff), lambda i: (layer, 0, 0)),
        _const_spec((None, d_ff, d), lambda i: (layer, 0, 0)),
    ]
    args = (x2, norm2, mods, mods, mods, w1, w2)
    if final:
        in_specs.append(pl.BlockSpec((1, d), lambda i: (0, 0)))
        args += (norm_f.reshape(1, d),)
    return pl.pallas_call(
        _mlp_final_kernel if final else _mlp_kernel,
        grid=(n_out // tm,),
        in_specs=in_specs,
        out_specs=pl.BlockSpec((tm, d), lambda i: (i, 0)),
        out_shape=jax.ShapeDtypeStruct((n_out, d), F32),
        compiler_params=_cparams(("parallel",)),
        name="mlp_final" if final else "mlp",
    )(*args)


def kernel(x, c, ctx, c_ctx, ada_w, ada_b, norm1, norm2, norm_f, mlp_w1, mlp_w2, hgrn_w_in, hgrn_lb,
           hgrn_gnorm, hgrn_w_out, conv_w_in, conv_w, conv_b, conv_w_out):
    bsz, seq, d = x.shape
    ctx_len = ctx.shape[1]
    depth = ada_w.shape[0]
    assert bsz <= CTX_MOD_ROW and d % HEAD_DIM == 0
    assert seq % max(TM_MLP, TM_CONV, T_SCAN) == 0 and ctx_len % T_SCAN == 0
    assert (bsz * ctx_len) % max(TM_MLP, TM_CONV) == 0
    assert ctx_len & (ctx_len - 1) == 0 and TM_CONV % ctx_len == 0
    rows = _Rows(bsz, seq, ctx_len, d)

    c8 = jnp.concatenate([c, c_ctx[None, :], jnp.zeros((MOD_ROWS - bsz - 1, d), F32)], axis=0)
    mods = _ada_table(c8, ada_w, ada_b).reshape(depth * MOD_ROWS * N_MODS, 1, d)
    lb = _lower_bounds(hgrn_lb).reshape(2, -1, 1, d)

    norm1 = norm1.reshape(depth, 1, d)
    norm2 = norm2.reshape(depth, 1, d)
    gnorm = hgrn_gnorm.reshape(-1, 1, d)
    conv_b = conv_b.reshape(-1, 1, d)
    w1, w2 = mlp_w1.astype(BF16), mlp_w2.astype(BF16)
    hw_in, hw_out = _prep_weight_pieces(hgrn_w_in, d), hgrn_w_out.astype(BF16)
    cw_in, cw_out = conv_w_in.astype(BF16), conv_w_out.astype(BF16)

    xs = (ctx.reshape(bsz * ctx_len, d), x.reshape(bsz * seq, d))
    for i in range(depth):
        j = i // N_MIXERS
        if i % N_MIXERS == 0:
            qsf, ksf, qsb, ksb, v, gate, dec = _hgrn_prep(rows, xs, norm1, mods, hw_in, lb, i, j)
            o_bwd = _hgrn_scan_bwd(rows, qsb, ksb, v, dec)
            x2 = _hgrn_scan_fwd(rows, qsf, ksf, v, dec, gate, o_bwd, xs, gnorm, mods, hw_out, i, j)
        else:
            x2 = _conv_mixer(rows, x2, norm1, mods, cw_in, conv_w, conv_b, cw_out, i, j,
                             along_rows=(j % 2 == 1))
        x2 = _mlp(rows, x2, norm2, mods, w1, w2, i, norm_f if i == depth - 1 else None)
        xs = (x2,)
    return x2.reshape(bsz, seq, d)
```

```python
import functools

import jax
import jax.numpy as jnp
from jax import lax
from jax.experimental import pallas as pl
from jax.experimental.pallas import tpu as pltpu

F32 = jnp.float32
BF16 = jnp.bfloat16

GRID_W = 64
HEAD_DIM = 128
CHUNK = 64
EPS = 1e-6
N_MIXERS = 2
CTX_MOD_ROW = 4
MOD_ROWS = 8
N_MODS = 6
DEC_ROWS = 8

VMEM_LIMIT = 56 * 1024 * 1024

TM_PREP = 512
PREP_STRIPS = 4
TM_MLP = 512
TM_CONV = 512
T_SCAN = 256
TN_ADA = 2048


def _cparams(sem):
    return pltpu.CompilerParams(dimension_semantics=sem, vmem_limit_bytes=VMEM_LIMIT)


def _const_spec(shape, index_map):
    return pl.BlockSpec(shape, index_map, pipeline_mode=pl.Buffered(1))


def _silu(t):
    return t * jax.nn.sigmoid(t)


def _rms_mod(x, gain, shift, scale):
    y = x * lax.rsqrt(jnp.mean(x * x, axis=-1, keepdims=True) + EPS) * gain
    return y * (1.0 + scale) + shift


def _ada_kernel(c_ref, w_ref, b_ref, o_ref):
    s = _silu(c_ref[...]).astype(BF16)
    o_ref[...] = jnp.dot(s, w_ref[...].astype(BF16), preferred_element_type=F32) + b_ref[...]


def _ada_table(c8, ada_w, ada_b):
    depth, d, n = ada_w.shape
    return pl.pallas_call(
        _ada_kernel,
        grid=(depth, n // TN_ADA),
        in_specs=[
            pl.BlockSpec((MOD_ROWS, d), lambda l, j: (0, 0)),
            pl.BlockSpec((None, d, TN_ADA), lambda l, j: (l, 0, j)),
            pl.BlockSpec((None, 1, TN_ADA), lambda l, j: (l, 0, j)),
        ],
        out_specs=pl.BlockSpec((None, MOD_ROWS, TN_ADA), lambda l, j: (l, 0, j)),
        out_shape=jax.ShapeDtypeStruct((depth, MOD_ROWS, n), F32),
        compiler_params=_cparams(("parallel", "parallel")),
        name="ada_table",
    )(c8, ada_w, ada_b.reshape(depth, 1, n))


def _lb_kernel(x_ref, o_ref):
    n_rec = x_ref.shape[1]
    for d in range(x_ref.shape[0]):
        rows = [x_ref[d, j:j + 1, :] for j in range(n_rec)]
        m = functools.reduce(jnp.maximum, rows)
        e = [jnp.exp(r - m) for r in rows]
        tot = functools.reduce(lambda a, b: a + b, e)
        acc = None
        for j in range(n_rec):
            p = e[j] / tot
            acc = p if acc is None else acc + p
            o_ref[d, j:j + 1, :] = acc - e[0] / tot


def _lower_bounds(hgrn_lb):
    return pl.pallas_call(
        _lb_kernel,
        out_shape=jax.ShapeDtypeStruct(hgrn_lb.shape, F32),
        name="hgrn_lower_bounds",
    )(hgrn_lb)


class _Rows:
    def __init__(self, bsz, seq, ctx_len, d):
        self.bsz, self.seq, self.ctx_len, self.d = bsz, seq, ctx_len, d
        self.n_ctx = bsz * ctx_len
        self.n_lat = bsz * seq
        self.n_rows = self.n_ctx + self.n_lat

    def mod_row(self, tm):
        n_ctx_tiles = self.n_ctx // tm
        per_batch = self.seq // tm
        return lambda i: jnp.where(i < n_ctx_tiles, CTX_MOD_ROW, (i - n_ctx_tiles) // per_batch)


def _mod_block(d, layer, k, row_of_grid):
    return pl.BlockSpec((None, 1, d),
                        lambda *g: ((layer * MOD_ROWS + row_of_grid(*g)) * N_MODS + k, 0, 0))


def _layer_vec_spec(d, layer):
    return pl.BlockSpec((None, 1, d), lambda *g: (layer, 0, 0))


def _cumsum_rows(g, reverse):
    n = g.shape[0]
    row = lax.broadcasted_iota(jnp.int32, g.shape, 0)
    s = 1
    while s < n:
        if reverse:
            g = g + jnp.where(row < n - s, pltpu.roll(g, n - s, axis=0), 0.0)
        else:
            g = g + jnp.where(row >= s, pltpu.roll(g, s, axis=0), 0.0)
        s *= 2
    return g


def _split_sources(n_src, refs):
    return refs[:n_src], refs[n_src:]


def _load_rows(x_refs, is_ctx):
    if len(x_refs) == 1:
        return x_refs[0][...]
    return jnp.where(is_ctx, x_refs[0][...], x_refs[1][...])


def _prep_kernel(n_src, ctx_tiles, *refs):
    x_refs, refs = _split_sources(n_src, refs)
    (gain_ref, sh_ref, sc_ref, w_ref, lbf_ref, lbb_ref,
     qsf_ref, ksf_ref, qsb_ref, ksb_ref, v_ref, gate_ref, dec_ref, za_ref, zb_ref, h_ref) = refs
    tm, d = v_ref.shape
    n_chunks = tm // CHUNK
    n_units = w_ref.shape[0]
    ws = d // n_units
    step = pl.program_id(0)
    outs = ((lbf_ref, qsf_ref, ksf_ref), (lbb_ref, qsb_ref, ksb_ref))
    assert w_ref.shape[-1] == 5 * ws

    @pl.when(step == 0)
    def _():
        zb_ref[...] = jnp.zeros(zb_ref.shape, F32)

    def body(zw_ref, zr_ref):
        x = _load_rows(x_refs, step < ctx_tiles)
        h_ref[...] = _rms_mod(x, gain_ref[...], sh_ref[...], sc_ref[...]).astype(BF16)
        dec_ref[:, 4:, :] = jnp.zeros((n_chunks, DEC_ROWS - 4, d), F32)
        for ci in range(n_chunks):
            if ci % (n_chunks // n_units) == 0:
                it = ci // (n_chunks // n_units)
                zw_ref[it] = jnp.dot(h_ref[...], w_ref[it], preferred_element_type=F32)
            rows = slice(ci * CHUNK, (ci + 1) * CHUNK)
            for j in range(n_units):
                cols = slice(j * ws, (j + 1) * ws)
                part = lambda p: zr_ref[j, rows, p * ws:(p + 1) * ws]
                v_ref[rows, cols] = part(2).astype(BF16)
                gate_ref[rows, cols] = _silu(part(4)).astype(BF16)
                q = _silu(part(3))
                for dirn, (lb_ref, qs_ref, ks_ref) in enumerate(outs):
                    lb = lb_ref[:, cols]
                    f = lb + (1.0 - lb) * jax.nn.sigmoid(part(dirn))
                    cum = _cumsum_rows(jnp.log(f), dirn == 1)
                    if dirn == 1:
                        ref, last = cum[CHUNK // 2:CHUNK // 2 + 1], cum[0:1]
                    else:
                        ref, last = cum[CHUNK // 2 - 1:CHUNK // 2], cum[CHUNK - 1:CHUNK]
                    qs_ref[rows, cols] = (q * jnp.exp(cum - ref)).astype(BF16)
                    ks_ref[rows, cols] = ((1.0 - f) * jnp.exp(ref - cum)).astype(BF16)
                    dec_ref[ci, 2 * dirn:2 * dirn + 1, cols] = jnp.exp(ref)
                    dec_ref[ci, 2 * dirn + 1:2 * dirn + 2, cols] = jnp.exp(last - ref)

    @pl.when(step % 2 == 0)
    def _():
        body(za_ref, zb_ref)

    @pl.when(step % 2 == 1)
    def _():
        body(zb_ref, za_ref)


def _source_specs(rows, xs, tm, tile_of_grid):
    d = rows.d
    if len(xs) == 1:
        return [pl.BlockSpec((tm, d), lambda *g: (tile_of_grid(*g), 0))]
    n_ctx_tiles = rows.n_ctx // tm
    return [pl.BlockSpec((tm, d), lambda *g: (jnp.minimum(tile_of_grid(*g), n_ctx_tiles - 1), 0)),
            pl.BlockSpec((tm, d), lambda *g: (jnp.maximum(tile_of_grid(*g) - n_ctx_tiles, 0), 0))]


def _prep_weight_pieces(w_in, d):
    n_layers, _, n_cols = w_in.shape
    n = PREP_STRIPS
    w = w_in.astype(BF16).reshape(n_layers, d, n_cols // d, n, d // n)
    return w.transpose(0, 3, 1, 2, 4).reshape(n_layers, n, d, n_cols // n)


def _hgrn_prep(rows, xs, norm1, mods, w_pieces, lb, layer, j):
    d = rows.d
    tm = TM_PREP
    n_pieces, _, piece_cols = w_pieces.shape[1:]
    n_tiles = rows.n_rows // tm
    cur = lambda s: jnp.minimum(s, n_tiles - 1)
    prev = lambda s: jnp.maximum(s - 1, 0)
    row = lambda s: rows.mod_row(tm)(cur(s))
    act_prev = pl.BlockSpec((tm, d), lambda s: (prev(s), 0))
    act_shape = jax.ShapeDtypeStruct((rows.n_rows, d), BF16)
    z_shape = pltpu.VMEM((n_pieces, tm, piece_cols), F32)
    return pl.pallas_call(
        functools.partial(_prep_kernel, len(xs), rows.n_ctx // tm),
        grid=(n_tiles + 1,),
        in_specs=_source_specs(rows, xs, tm, cur) + [
            _layer_vec_spec(d, layer),
            _mod_block(d, layer, 0, row),
            _mod_block(d, layer, 1, row),
            _const_spec((None, n_pieces, d, piece_cols), lambda s: (j, 0, 0, 0)),
            pl.BlockSpec((None, None, 1, d), lambda s: (0, j, 0, 0)),
            pl.BlockSpec((None, None, 1, d), lambda s: (1, j, 0, 0)),
        ],
        out_specs=[act_prev] * 6
        + [pl.BlockSpec((tm // CHUNK, DEC_ROWS, d), lambda s: (prev(s), 0, 0))],
        out_shape=[act_shape] * 6 + [jax.ShapeDtypeStruct((rows.n_rows // CHUNK, DEC_ROWS, d), F32)],
        scratch_shapes=[z_shape, z_shape, pltpu.VMEM((tm, d), BF16)],
        compiler_params=_cparams(("arbitrary",)),
        name="hgrn_prep",
    )(*xs, norm1, mods, mods, w_pieces, lb, lb)


def _gla_tile(qs_ref, ks_ref, v_ref, dec_ref, s_ref, dirn, emit):
    t_rows, d = qs_ref.shape
    reverse = dirn == 1
    n_chunks = t_rows // CHUNK
    r = lax.broadcasted_iota(jnp.int32, (CHUNK, CHUNK), 0)
    c = lax.broadcasted_iota(jnp.int32, (CHUNK, CHUNK), 1)
    mask = (c >= r) if reverse else (c <= r)
    nt = (((1,), (1,)), ((), ()))
    tn = (((0,), (0,)), ((), ()))
    order = range(n_chunks - 1, -1, -1) if reverse else range(n_chunks)
    heads = range(d // HEAD_DIM)
    hsl = lambda h: slice(h * HEAD_DIM, (h + 1) * HEAD_DIM)
    rsl = lambda ci: slice(ci * CHUNK, (ci + 1) * CHUNK)
    p, u = {}, {}
    for h in heads:
        for ci in order:
            sc = lax.dot_general(qs_ref[rsl(ci), hsl(h)], ks_ref[rsl(ci), hsl(h)], nt,
                                 preferred_element_type=F32)
            p[h, ci] = jnp.where(mask, sc, 0.0).astype(BF16)
    for h in heads:
        for ci in order:
            u[h, ci] = lax.dot_general(v_ref[rsl(ci), hsl(h)], ks_ref[rsl(ci), hsl(h)], tn,
                                       preferred_element_type=F32)
    for h in heads:
        st = s_ref[h]
        for ci in order:
            e_ref = dec_ref[ci, 2 * dirn:2 * dirn + 1, hsl(h)]
            e_lr = dec_ref[ci, 2 * dirn + 1:2 * dirn + 2, hsl(h)]
            o = lax.dot_general(qs_ref[rsl(ci), hsl(h)], (st * e_ref).astype(BF16), nt,
                                preferred_element_type=F32)
            o = o + jnp.dot(p[h, ci], v_ref[rsl(ci), hsl(h)], preferred_element_type=F32)
            st = st * (e_ref * e_lr) + u[h, ci] * e_lr
            emit(rsl(ci), hsl(h), o)
        s_ref[h] = st


def _scan_bwd_kernel(qs_ref, ks_ref, v_ref, dec_ref, o_ref, s_ref):
    @pl.when(pl.program_id(1) == 0)
    def _():
        s_ref[...] = jnp.zeros_like(s_ref)

    def emit(rows, hs, o):
        o_ref[rows, hs] = o

    _gla_tile(qs_ref, ks_ref, v_ref, dec_ref, s_ref, 1, emit)


def _scan_fwd_kernel(n_src, ctx_steps, *refs):
    x_refs, refs = _split_sources(n_src, refs)
    (qs_ref, ks_ref, v_ref, dec_ref, gate_ref, ob_ref, gn_ref, g1_ref, wout_ref,
     o_ref, s_ref, y_ref) = refs

    @pl.when(pl.program_id(1) == 0)
    def _():
        s_ref[...] = jnp.zeros_like(s_ref)

    def emit(rows, hs, o):
        o = o + ob_ref[rows, hs]
        o = o * lax.rsqrt(jnp.mean(o * o, axis=-1, keepdims=True) + EPS)
        y_ref[rows, hs] = (o * gn_ref[:, hs] * gate_ref[rows, hs].astype(F32)).astype(BF16)

    _gla_tile(qs_ref, ks_ref, v_ref, dec_ref, s_ref, 0, emit)
    y = jnp.dot(y_ref[...], wout_ref[...], preferred_element_type=F32)
    o_ref[...] = _load_rows(x_refs, pl.program_id(1) < ctx_steps) + g1_ref[...] * y


def _scan_tiles(rows, reverse):
    t = T_SCAN
    nct = rows.ctx_len // t
    ntl = rows.seq // t
    ctx_tiles = rows.n_ctx // t

    def tile(b, s):
        if reverse:
            return jnp.where(s < nct, b * nct + (nct - 1 - s), ctx_tiles + b * ntl + (ntl - 1 - (s - nct)))
        return jnp.where(s < nct, b * nct + s, ctx_tiles + b * ntl + (s - nct))

    mod_row = lambda b, s: jnp.where(s < nct, CTX_MOD_ROW, b)
    return tile, mod_row, nct + ntl


def _hgrn_scan_bwd(rows, qs, ks, v, dec):
    d = rows.d
    t = T_SCAN
    tile, _, steps = _scan_tiles(rows, True)
    act = pl.BlockSpec((t, d), lambda b, s: (tile(b, s), 0))
    return pl.pallas_call(
        _scan_bwd_kernel,
        grid=(rows.bsz, steps),
        in_specs=[act, act, act,
                  pl.BlockSpec((t // CHUNK, DEC_ROWS, d), lambda b, s: (tile(b, s), 0, 0))],
        out_specs=act,
        out_shape=jax.ShapeDtypeStruct((rows.n_rows, d), F32),
        scratch_shapes=[pltpu.VMEM((d // HEAD_DIM, HEAD_DIM, HEAD_DIM), F32)],
        compiler_params=_cparams(("arbitrary", "arbitrary")),
        name="hgrn_scan_bwd",
    )(qs, ks, v, dec)


def _hgrn_scan_fwd(rows, qs, ks, v, dec, gate, o_bwd, xs, gnorm, mods, w_out, layer, j):
    d = rows.d
    t = T_SCAN
    tile, mod_row, steps = _scan_tiles(rows, False)
    act = pl.BlockSpec((t, d), lambda b, s: (tile(b, s), 0))
    return pl.pallas_call(
        functools.partial(_scan_fwd_kernel, len(xs), rows.ctx_len // t),
        grid=(rows.bsz, steps),
        in_specs=_source_specs(rows, xs, t, tile) + [
                  act, act, act,
                  pl.BlockSpec((t // CHUNK, DEC_ROWS, d), lambda b, s: (tile(b, s), 0, 0)),
                  act, act,
                  _layer_vec_spec(d, j),
                  _mod_block(d, layer, 2, mod_row),
                  _const_spec((None, d, d), lambda b, s: (j, 0, 0))],
        out_specs=act,
        out_shape=jax.ShapeDtypeStruct((rows.n_rows, d), F32),
        scratch_shapes=[pltpu.VMEM((d // HEAD_DIM, HEAD_DIM, HEAD_DIM), F32),
                        pltpu.VMEM((t, d), BF16)],
        compiler_params=_cparams(("arbitrary", "arbitrary")),
        name="hgrn_scan_fwd",
    )(*xs, qs, ks, v, dec, gate, o_bwd, gnorm, mods, w_out)


def _conv_taps(cw_ref, cb_ref, um, u, up):
    return cb_ref[...] + cw_ref[0:1, :] * um + cw_ref[1:2, :] * u + cw_ref[2:3, :] * up


def _conv_seq_kernel(n_ctx_tiles, ctx_len, x_ref, gain_ref, sh_ref, sc_ref, g1_ref, win_ref,
                     cw_ref, cb_ref, wout_ref, o_ref):
    x = x_ref[...]
    tm, d = x.shape
    h = _rms_mod(x, gain_ref[...], sh_ref[...], sc_ref[...]).astype(BF16)
    p = jnp.dot(h, win_ref[...], preferred_element_type=F32)
    gate_b, u = p[:, :d], p[:, d:2 * d] * p[:, 2 * d:]
    period = jnp.where(pl.program_id(0) < n_ctx_tiles, ctx_len, GRID_W)
    pos = lax.broadcasted_iota(jnp.int32, (tm, 1), 0) & (period - 1)
    um = jnp.where(pos == 0, 0.0, pltpu.roll(u, 1, axis=0))
    up = jnp.where(pos == period - 1, 0.0, pltpu.roll(u, tm - 1, axis=0))
    cv = _conv_taps(cw_ref, cb_ref, um, u, up)
    y = jnp.dot((gate_b * cv).astype(BF16), wout_ref[...], preferred_element_type=F32)
    o_ref[...] = x + g1_ref[...] * y


def _conv_rows_kernel(n_ctx_tiles, tiles_per_batch, xp_ref, x_ref, xn_ref, gain_ref, sh_ref,
                      sc_ref, g1_ref, win_ref, cw_ref, cb_ref, wout_ref, o_ref, h_ref):
    x = x_ref[...]
    tm, d = x.shape
    gain, sh, sc = gain_ref[...], sh_ref[...], sc_ref[...]
    h_ref[0:GRID_W, :] = _rms_mod(xp_ref[...], gain, sh, sc).astype(BF16)
    h_ref[GRID_W:GRID_W + tm, :] = _rms_mod(x, gain, sh, sc).astype(BF16)
    h_ref[GRID_W + tm:, :] = _rms_mod(xn_ref[...], gain, sh, sc).astype(BF16)
    p = jnp.dot(h_ref[...], win_ref[:, d:], preferred_element_type=F32)
    u = p[:, :d] * p[:, d:]
    gate_b = jnp.dot(h_ref[GRID_W:GRID_W + tm, :], win_ref[:, :d], preferred_element_type=F32)
    i = pl.program_id(0)
    jl = (i - n_ctx_tiles) % tiles_per_batch
    is_lat = i >= n_ctx_tiles
    has_prev = jnp.logical_and(is_lat, jl != 0)
    has_next = jnp.logical_and(is_lat, jl != tiles_per_batch - 1)
    row = lax.broadcasted_iota(jnp.int32, (tm, 1), 0)
    um = jnp.where(jnp.logical_or(row >= GRID_W, has_prev), u[0:tm], 0.0)
    up = jnp.where(jnp.logical_or(row < tm - GRID_W, has_next), u[2 * GRID_W:], 0.0)
    cv = _conv_taps(cw_ref, cb_ref, um, u[GRID_W:GRID_W + tm], up)
    y = jnp.dot((gate_b * cv).astype(BF16), wout_ref[...], preferred_element_type=F32)
    o_ref[...] = x + g1_ref[...] * y


def _conv_mixer(rows, x2, norm1, mods, w_in, cw, cb, w_out, layer, j, along_rows):
    d = rows.d
    tm = TM_CONV
    row = rows.mod_row(tm)
    n_tiles = rows.n_rows // tm
    n_ctx_tiles = rows.n_ctx // tm
    common = [
        _layer_vec_spec(d, layer),
        _mod_block(d, layer, 0, row),
        _mod_block(d, layer, 1, row),
        _mod_block(d, layer, 2, row),
        _const_spec((None, d, 3 * d), lambda i: (j, 0, 0)),
        pl.BlockSpec((None, 3, d), lambda i: (j, 0, 0)),
        _layer_vec_spec(d, j),
        _const_spec((None, d, d), lambda i: (j, 0, 0)),
    ]
    args = (norm1, mods, mods, mods, w_in, cw, cb, w_out)
    if along_rows:
        sub = tm // GRID_W
        last = rows.n_rows // GRID_W - 1
        kern = functools.partial(_conv_rows_kernel, n_ctx_tiles, rows.seq // tm)
        in_specs = [pl.BlockSpec((GRID_W, d), lambda i: (jnp.maximum(i * sub - 1, 0), 0)),
                    pl.BlockSpec((tm, d), lambda i: (i, 0)),
                    pl.BlockSpec((GRID_W, d), lambda i: (jnp.minimum((i + 1) * sub, last), 0))]
        args = (x2, x2, x2) + args
        scratch = [pltpu.VMEM((tm + 2 * GRID_W, d), BF16)]
        name = "conv_mixer_rows"
    else:
        kern = functools.partial(_conv_seq_kernel, n_ctx_tiles, rows.ctx_len)
        in_specs = [pl.BlockSpec((tm, d), lambda i: (i, 0))]
        args = (x2,) + args
        scratch = []
        name = "conv_mixer_seq"
    return pl.pallas_call(
        kern,
        grid=(n_tiles,),
        in_specs=in_specs + common,
        out_specs=pl.BlockSpec((tm, d), lambda i: (i, 0)),
        out_shape=jax.ShapeDtypeStruct((rows.n_rows, d), F32),
        scratch_shapes=scratch,
        compiler_params=_cparams(("parallel",)),
        name=name,
    )(*args)


def _mlp_residual(x_ref, gain_ref, sh_ref, sc_ref, g2_ref, w1_ref, w2_ref):
    x = x_ref[...]
    h = _rms_mod(x, gain_ref[...], sh_ref[...], sc_ref[...]).astype(BF16)
    a = jnp.dot(h, w1_ref[...], preferred_element_type=F32)
    a = jnp.square(jnp.maximum(a, 0.0)).astype(BF16)
    y = jnp.dot(a, w2_ref[...], preferred_element_type=F32)
    return x + g2_ref[...] * y


def _mlp_kernel(x_ref, gain_ref, sh_ref, sc_ref, g2_ref, w1_ref, w2_ref, o_ref):
    o_ref[...] = _mlp_residual(x_ref, gain_ref, sh_ref, sc_ref, g2_ref, w1_ref, w2_ref)


def _mlp_final_kernel(x_ref, gain_ref, sh_ref, sc_ref, g2_ref, w1_ref, w2_ref, gf_ref, o_ref):
    x = _mlp_residual(x_ref, gain_ref, sh_ref, sc_ref, g2_ref, w1_ref, w2_ref)
    o_ref[...] = x * lax.rsqrt(jnp.mean(x * x, axis=-1, keepdims=True) + EPS) * gf_ref[...]


def _mlp(rows, x2, norm2, mods, w1, w2, layer, norm_f=None):
    d = rows.d
    d_ff = w1.shape[-1]
    tm = TM_MLP
    final = norm_f is not None
    first = rows.n_ctx // tm if final else 0
    n_out = rows.n_lat if final else rows.n_rows
    row = lambda i: rows.mod_row(tm)(i + first)
    in_specs = [
        pl.BlockSpec((tm, d), lambda i: (i + first, 0)),
        _layer_vec_spec(d, layer),
        _mod_block(d, layer, 3, row),
        _mod_block(d, layer, 4, row),
        _mod_block(d, layer, 5, row),
        _const_spec((None, d, d_ff), lambda i: (layer, 0, 0)),
        _const_spec((None, d_ff, d), lambda i: (layer, 0, 0)),
    ]
    args = (x2, norm2, mods, mods, mods, w1, w2)
    if final:
        in_specs.append(pl.BlockSpec((1, d), lambda i: (0, 0)))
        args += (norm_f.reshape(1, d),)
    return pl.pallas_call(
        _mlp_final_kernel if final else _mlp_kernel,
        grid=(n_out // tm,),
        in_specs=in_specs,
        out_specs=pl.BlockSpec((tm, d), lambda i: (i, 0)),
        out_shape=jax.ShapeDtypeStruct((n_out, d), F32),
        compiler_params=_cparams(("parallel",)),
        name="mlp_final" if final else "mlp",
    )(*args)


def kernel(x, c, ctx, c_ctx, ada_w, ada_b, norm1, norm2, norm_f, mlp_w1, mlp_w2, hgrn_w_in, hgrn_lb,
           hgrn_gnorm, hgrn_w_out, conv_w_in, conv_w, conv_b, conv_w_out):
    bsz, seq, d = x.shape
    ctx_len = ctx.shape[1]
    depth = ada_w.shape[0]
    assert bsz <= CTX_MOD_ROW and d % HEAD_DIM == 0
    assert seq % max(TM_MLP, TM_CONV, T_SCAN) == 0 and ctx_len % T_SCAN == 0
    assert (bsz * ctx_len) % max(TM_MLP, TM_CONV) == 0
    assert ctx_len & (ctx_len - 1) == 0 and TM_CONV % ctx_len == 0
    rows = _Rows(bsz, seq, ctx_len, d)

    c8 = jnp.concatenate([c, c_ctx[None, :], jnp.zeros((MOD_ROWS - bsz - 1, d), F32)], axis=0)
    mods = _ada_table(c8, ada_w, ada_b).reshape(depth * MOD_ROWS * N_MODS, 1, d)
    lb = _lower_bounds(hgrn_lb).reshape(2, -1, 1, d)

    norm1 = norm1.reshape(depth, 1, d)
    norm2 = norm2.reshape(depth, 1, d)
    gnorm = hgrn_gnorm.reshape(-1, 1, d)
    conv_b = conv_b.reshape(-1, 1, d)
    w1, w2 = mlp_w1.astype(BF16), mlp_w2.astype(BF16)
    hw_in, hw_out = _prep_weight_pieces(hgrn_w_in, d), hgrn_w_out.astype(BF16)
    cw_in, cw_out = conv_w_in.astype(BF16), conv_w_out.astype(BF16)

    xs = (ctx.reshape(bsz * ctx_len, d), x.reshape(bsz * seq, d))
    for i in range(depth):
        j = i // N_MIXERS
        if i % N_MIXERS == 0:
            qsf, ksf, qsb, ksb, v, gate, dec = _hgrn_prep(rows, xs, norm1, mods, hw_in, lb, i, j)
            o_bwd = _hgrn_scan_bwd(rows, qsb, ksb, v, dec)
            x2 = _hgrn_scan_fwd(rows, qsf, ksf, v, dec, gate, o_bwd, xs, gnorm, mods, hw_out, i, j)
        else:
            x2 = _conv_mixer(rows, x2, norm1, mods, cw_in, conv_w, conv_b, cw_out, i, j,
                             along_rows=(j % 2 == 1))
        x2 = _mlp(rows, x2, norm2, mods, w1, w2, i, norm_f if i == depth - 1 else None)
        xs = (x2,)
    return x2.reshape(bsz, seq, d)
```

```python
import functools

import jax
import jax.numpy as jnp
from jax import lax
from jax.experimental import pallas as pl
from jax.experimental.pallas import tpu as pltpu

F32 = jnp.float32
BF16 = jnp.bfloat16

GRID_W = 64
HEAD_DIM = 128
CHUNK = 64
EPS = 1e-6
N_MIXERS = 2
CTX_MOD_ROW = 4
MOD_ROWS = 8
N_MODS = 6
DEC_ROWS = 8

VMEM_LIMIT = 56 * 1024 * 1024

TM_PREP = 512
PREP_PIECES = 4
TM_MLP = 512
TM_CONV = 1024
T_SCAN = 256
TN_ADA = 2048


def _cparams(sem):
    return pltpu.CompilerParams(dimension_semantics=sem, vmem_limit_bytes=VMEM_LIMIT)


def _const_spec(shape, index_map):
    return pl.BlockSpec(shape, index_map, pipeline_mode=pl.Buffered(1))


def _silu(t):
    return t * jax.nn.sigmoid(t)


def _rms_mod(x, gain, shift, scale):
    y = x * lax.rsqrt(jnp.mean(x * x, axis=-1, keepdims=True) + EPS) * gain
    return y * (1.0 + scale) + shift


def _ada_kernel(c_ref, w_ref, b_ref, o_ref):
    s = _silu(c_ref[...]).astype(BF16)
    o_ref[...] = jnp.dot(s, w_ref[...].astype(BF16), preferred_element_type=F32) + b_ref[...]


def _ada_table(c8, ada_w, ada_b):
    depth, d, n = ada_w.shape
    return pl.pallas_call(
        _ada_kernel,
        grid=(depth, n // TN_ADA),
        in_specs=[
            pl.BlockSpec((MOD_ROWS, d), lambda l, j: (0, 0)),
            pl.BlockSpec((None, d, TN_ADA), lambda l, j: (l, 0, j)),
            pl.BlockSpec((None, 1, TN_ADA), lambda l, j: (l, 0, j)),
        ],
        out_specs=pl.BlockSpec((None, MOD_ROWS, TN_ADA), lambda l, j: (l, 0, j)),
        out_shape=jax.ShapeDtypeStruct((depth, MOD_ROWS, n), F32),
        compiler_params=_cparams(("parallel", "parallel")),
        name="ada_table",
    )(c8, ada_w, ada_b.reshape(depth, 1, n))


def _lb_kernel(x_ref, o_ref):
    n_rec = x_ref.shape[1]
    for d in range(x_ref.shape[0]):
        rows = [x_ref[d, j:j + 1, :] for j in range(n_rec)]
        m = functools.reduce(jnp.maximum, rows)
        e = [jnp.exp(r - m) for r in rows]
        tot = functools.reduce(lambda a, b: a + b, e)
        acc = None
        for j in range(n_rec):
            p = e[j] / tot
            acc = p if acc is None else acc + p
            o_ref[d, j:j + 1, :] = acc - e[0] / tot


def _lower_bounds(hgrn_lb):
    return pl.pallas_call(
        _lb_kernel,
        out_shape=jax.ShapeDtypeStruct(hgrn_lb.shape, F32),
        name="hgrn_lower_bounds",
    )(hgrn_lb)


class _Rows:
    def __init__(self, bsz, seq, ctx_len, d):
        self.bsz, self.seq, self.ctx_len, self.d = bsz, seq, ctx_len, d
        self.n_ctx = bsz * ctx_len
        self.n_lat = bsz * seq
        self.n_rows = self.n_ctx + self.n_lat

    def mod_row(self, tm):
        n_ctx_tiles = self.n_ctx // tm
        per_batch = self.seq // tm
        return lambda i: jnp.where(i < n_ctx_tiles, CTX_MOD_ROW, (i - n_ctx_tiles) // per_batch)


def _mod_block(d, layer, k, row_of_grid):
    return pl.BlockSpec((None, 1, d),
                        lambda *g: ((layer * MOD_ROWS + row_of_grid(*g)) * N_MODS + k, 0, 0))


def _layer_vec_spec(d, layer):
    return pl.BlockSpec((None, 1, d), lambda *g: (layer, 0, 0))


def _cumsum_rows(g, reverse):
    n = g.shape[0]
    row = lax.broadcasted_iota(jnp.int32, g.shape, 0)
    s = 1
    while s < n:
        if reverse:
            g = g + jnp.where(row < n - s, pltpu.roll(g, n - s, axis=0), 0.0)
        else:
            g = g + jnp.where(row >= s, pltpu.roll(g, s, axis=0), 0.0)
        s *= 2
    return g


def _split_sources(n_src, refs):
    return refs[:n_src], refs[n_src:]


def _load_rows(x_refs, is_ctx):
    if len(x_refs) == 1:
        return x_refs[0][...]
    return jnp.where(is_ctx, x_refs[0][...], x_refs[1][...])


def _prep_kernel(n_src, ctx_tiles, *refs):
    x_refs, refs = _split_sources(n_src, refs)
    (gain_ref, sh_ref, sc_ref, w_ref, lbf_ref, lbb_ref,
     qsf_ref, ksf_ref, qsb_ref, ksb_ref, v_ref, gate_ref, dec_ref, za_ref, zb_ref, h_ref) = refs
    tm, d = v_ref.shape
    n_chunks = tm // CHUNK
    n_units = za_ref.shape[0]
    wpiece = za_ref.shape[-1]
    n_parts = n_units * wpiece // d
    ws = d // n_units
    step = pl.program_id(0)
    outs = ((lbf_ref, qsf_ref, ksf_ref), (lbb_ref, qsb_ref, ksb_ref))
    assert n_parts == 5 and wpiece == n_parts * ws

    @pl.when(step == 0)
    def _():
        zb_ref[...] = jnp.zeros(zb_ref.shape, F32)

    def body(zw_ref, zr_ref):
        x = _load_rows(x_refs, step < ctx_tiles)
        h_ref[...] = _rms_mod(x, gain_ref[...], sh_ref[...], sc_ref[...]).astype(BF16)
        dec_ref[:, 4:, :] = jnp.zeros((n_chunks, DEC_ROWS - 4, d), F32)
        for ci in range(n_chunks):
            if ci % (n_chunks // n_units) == 0:
                it = ci // (n_chunks // n_units)
                zw_ref[it] = jnp.dot(h_ref[...], w_ref[it], preferred_element_type=F32)
            rows = slice(ci * CHUNK, (ci + 1) * CHUNK)
            for j in range(n_units):
                cols = slice(j * ws, (j + 1) * ws)

                def part(p):
                    piece, k = divmod(p * n_units + j, n_parts)
                    return zr_ref[piece, rows, k * ws:(k + 1) * ws]

                v_ref[rows, cols] = part(2).astype(BF16)
                gate_ref[rows, cols] = _silu(part(4)).astype(BF16)
                q = _silu(part(3))
                for dirn, (lb_ref, qs_ref, ks_ref) in enumerate(outs):
                    lb = lb_ref[:, cols]
                    f = lb + (1.0 - lb) * jax.nn.sigmoid(part(dirn))
                    cum = _cumsum_rows(jnp.log(f), dirn == 1)
                    if dirn == 1:
                        ref, last = cum[CHUNK // 2:CHUNK // 2 + 1], cum[0:1]
                    else:
                        ref, last = cum[CHUNK // 2 - 1:CHUNK // 2], cum[CHUNK - 1:CHUNK]
                    qs_ref[rows, cols] = (q * jnp.exp(cum - ref)).astype(BF16)
                    ks_ref[rows, cols] = ((1.0 - f) * jnp.exp(ref - cum)).astype(BF16)
                    dec_ref[ci, 2 * dirn:2 * dirn + 1, cols] = jnp.exp(ref)
                    dec_ref[ci, 2 * dirn + 1:2 * dirn + 2, cols] = jnp.exp(last - ref)

    @pl.when(step % 2 == 0)
    def _():
        body(za_ref, zb_ref)

    @pl.when(step % 2 == 1)
    def _():
        body(zb_ref, za_ref)


def _source_specs(rows, xs, tm, tile_of_grid):
    d = rows.d
    if len(xs) == 1:
        return [pl.BlockSpec((tm, d), lambda *g: (tile_of_grid(*g), 0))]
    n_ctx_tiles = rows.n_ctx // tm
    return [pl.BlockSpec((tm, d), lambda *g: (jnp.minimum(tile_of_grid(*g), n_ctx_tiles - 1), 0)),
            pl.BlockSpec((tm, d), lambda *g: (jnp.maximum(tile_of_grid(*g) - n_ctx_tiles, 0), 0))]


def _prep_weight_pieces(w_in):
    n_layers, d, n = w_in.shape
    w = w_in.astype(BF16).reshape(n_layers, d, PREP_PIECES, n // PREP_PIECES)
    return w.transpose(0, 2, 1, 3)


def _hgrn_prep(rows, xs, norm1, mods, w_pieces, lb, layer, j):
    d = rows.d
    tm = TM_PREP
    n_pieces, _, piece_cols = w_pieces.shape[1:]
    n_tiles = rows.n_rows // tm
    cur = lambda s: jnp.minimum(s, n_tiles - 1)
    prev = lambda s: jnp.maximum(s - 1, 0)
    row = lambda s: rows.mod_row(tm)(cur(s))
    act_prev = pl.BlockSpec((tm, d), lambda s: (prev(s), 0))
    act_shape = jax.ShapeDtypeStruct((rows.n_rows, d), BF16)
    z_shape = pltpu.VMEM((n_pieces, tm, piece_cols), F32)
    return pl.pallas_call(
        functools.partial(_prep_kernel, len(xs), rows.n_ctx // tm),
        grid=(n_tiles + 1,),
        in_specs=_source_specs(rows, xs, tm, cur) + [
            _layer_vec_spec(d, layer),
            _mod_block(d, layer, 0, row),
            _mod_block(d, layer, 1, row),
            _const_spec((None, n_pieces, d, piece_cols), lambda s: (j, 0, 0, 0)),
            pl.BlockSpec((None, None, 1, d), lambda s: (0, j, 0, 0)),
            pl.BlockSpec((None, None, 1, d), lambda s: (1, j, 0, 0)),
        ],
        out_specs=[act_prev] * 6
        + [pl.BlockSpec((tm // CHUNK, DEC_ROWS, d), lambda s: (prev(s), 0, 0))],
        out_shape=[act_shape] * 6 + [jax.ShapeDtypeStruct((rows.n_rows // CHUNK, DEC_ROWS, d), F32)],
        scratch_shapes=[z_shape, z_shape, pltpu.VMEM((tm, d), BF16)],
        compiler_params=_cparams(("arbitrary",)),
        name="hgrn_prep",
    )(*xs, norm1, mods, mods, w_pieces, lb, lb)


def _gla_tile(qs_ref, ks_ref, v_ref, dec_ref, s_ref, dirn, emit):
    t_rows, d = qs_ref.shape
    reverse = dirn == 1
    n_chunks = t_rows // CHUNK
    r = lax.broadcasted_iota(jnp.int32, (CHUNK, CHUNK), 0)
    c = lax.broadcasted_iota(jnp.int32, (CHUNK, CHUNK), 1)
    mask = (c >= r) if reverse else (c <= r)
    nt = (((1,), (1,)), ((), ()))
    tn = (((0,), (0,)), ((), ()))
    order = range(n_chunks - 1, -1, -1) if reverse else range(n_chunks)
    heads = range(d // HEAD_DIM)
    hsl = lambda h: slice(h * HEAD_DIM, (h + 1) * HEAD_DIM)
    rsl = lambda ci: slice(ci * CHUNK, (ci + 1) * CHUNK)
    p, u = {}, {}
    for h in heads:
        for ci in order:
            sc = lax.dot_general(qs_ref[rsl(ci), hsl(h)], ks_ref[rsl(ci), hsl(h)], nt,
                                 preferred_element_type=F32)
            p[h, ci] = jnp.where(mask, sc, 0.0).astype(BF16)
    for h in heads:
        for ci in order:
            u[h, ci] = lax.dot_general(v_ref[rsl(ci), hsl(h)], ks_ref[rsl(ci), hsl(h)], tn,
                                       preferred_element_type=F32)
    for h in heads:
        st = s_ref[h]
        for ci in order:
            e_ref = dec_ref[ci, 2 * dirn:2 * dirn + 1, hsl(h)]
            e_lr = dec_ref[ci, 2 * dirn + 1:2 * dirn + 2, hsl(h)]
            o = lax.dot_general(qs_ref[rsl(ci), hsl(h)], (st * e_ref).astype(BF16), nt,
                                preferred_element_type=F32)
            o = o + jnp.dot(p[h, ci], v_ref[rsl(ci), hsl(h)], preferred_element_type=F32)
            st = st * (e_ref * e_lr) + u[h, ci] * e_lr
            emit(rsl(ci), hsl(h), o)
        s_ref[h] = st


def _scan_bwd_kernel(qs_ref, ks_ref, v_ref, dec_ref, o_ref, s_ref):
    @pl.when(pl.program_id(1) == 0)
    def _():
        s_ref[...] = jnp.zeros_like(s_ref)

    def emit(rows, hs, o):
        o_ref[rows, hs] = o

    _gla_tile(qs_ref, ks_ref, v_ref, dec_ref, s_ref, 1, emit)


def _scan_fwd_kernel(n_src, ctx_steps, *refs):
    x_refs, refs = _split_sources(n_src, refs)
    (qs_ref, ks_ref, v_ref, dec_ref, gate_ref, ob_ref, gn_ref, g1_ref, wout_ref,
     o_ref, s_ref, y_ref) = refs

    @pl.when(pl.program_id(1) == 0)
    def _():
        s_ref[...] = jnp.zeros_like(s_ref)

    def emit(rows, hs, o):
        o = o + ob_ref[rows, hs]
        o = o * lax.rsqrt(jnp.mean(o * o, axis=-1, keepdims=True) + EPS)
        y_ref[rows, hs] = (o * gn_ref[:, hs] * gate_ref[rows, hs].astype(F32)).astype(BF16)

    _gla_tile(qs_ref, ks_ref, v_ref, dec_ref, s_ref, 0, emit)
    y = jnp.dot(y_ref[...], wout_ref[...], preferred_element_type=F32)
    o_ref[...] = _load_rows(x_refs, pl.program_id(1) < ctx_steps) + g1_ref[...] * y


def _scan_tiles(rows, reverse):
    t = T_SCAN
    nct = rows.ctx_len // t
    ntl = rows.seq // t
    ctx_tiles = rows.n_ctx // t

    def tile(b, s):
        if reverse:
            return jnp.where(s < nct, b * nct + (nct - 1 - s), ctx_tiles + b * ntl + (ntl - 1 - (s - nct)))
        return jnp.where(s < nct, b * nct + s, ctx_tiles + b * ntl + (s - nct))

    mod_row = lambda b, s: jnp.where(s < nct, CTX_MOD_ROW, b)
    return tile, mod_row, nct + ntl


def _hgrn_scan_bwd(rows, qs, ks, v, dec):
    d = rows.d
    t = T_SCAN
    tile, _, steps = _scan_tiles(rows, True)
    act = pl.BlockSpec((t, d), lambda b, s: (tile(b, s), 0))
    return pl.pallas_call(
        _scan_bwd_kernel,
        grid=(rows.bsz, steps),
        in_specs=[act, act, act,
                  pl.BlockSpec((t // CHUNK, DEC_ROWS, d), lambda b, s: (tile(b, s), 0, 0))],
        out_specs=act,
        out_shape=jax.ShapeDtypeStruct((rows.n_rows, d), F32),
        scratch_shapes=[pltpu.VMEM((d // HEAD_DIM, HEAD_DIM, HEAD_DIM), F32)],
        compiler_params=_cparams(("arbitrary", "arbitrary")),
        name="hgrn_scan_bwd",
    )(qs, ks, v, dec)


def _hgrn_scan_fwd(rows, qs, ks, v, dec, gate, o_bwd, xs, gnorm, mods, w_out, layer, j):
    d = rows.d
    t = T_SCAN
    tile, mod_row, steps = _scan_tiles(rows, False)
    act = pl.BlockSpec((t, d), lambda b, s: (tile(b, s), 0))
    return pl.pallas_call(
        functools.partial(_scan_fwd_kernel, len(xs), rows.ctx_len // t),
        grid=(rows.bsz, steps),
        in_specs=_source_specs(rows, xs, t, tile) + [
                  act, act, act,
                  pl.BlockSpec((t // CHUNK, DEC_ROWS, d), lambda b, s: (tile(b, s), 0, 0)),
                  act, act,
                  _layer_vec_spec(d, j),
                  _mod_block(d, layer, 2, mod_row),
                  _const_spec((None, d, d), lambda b, s: (j, 0, 0))],
        out_specs=act,
        out_shape=jax.ShapeDtypeStruct((rows.n_rows, d), F32),
        scratch_shapes=[pltpu.VMEM((d // HEAD_DIM, HEAD_DIM, HEAD_DIM), F32),
                        pltpu.VMEM((t, d), BF16)],
        compiler_params=_cparams(("arbitrary", "arbitrary")),
        name="hgrn_scan_fwd",
    )(*xs, qs, ks, v, dec, gate, o_bwd, gnorm, mods, w_out)


def _conv_taps(cw_ref, cb_ref, um, u, up):
    return cb_ref[...] + cw_ref[0:1, :] * um + cw_ref[1:2, :] * u + cw_ref[2:3, :] * up


def _conv_seq_kernel(n_ctx_tiles, ctx_len, x_ref, gain_ref, sh_ref, sc_ref, g1_ref, win_ref,
                     cw_ref, cb_ref, wout_ref, o_ref):
    x = x_ref[...]
    tm, d = x.shape
    h = _rms_mod(x, gain_ref[...], sh_ref[...], sc_ref[...]).astype(BF16)
    p = jnp.dot(h, win_ref[...], preferred_element_type=F32)
    gate_b, u = p[:, :d], p[:, d:2 * d] * p[:, 2 * d:]
    period = jnp.where(pl.program_id(0) < n_ctx_tiles, ctx_len, GRID_W)
    pos = lax.broadcasted_iota(jnp.int32, (tm, 1), 0) & (period - 1)
    um = jnp.where(pos == 0, 0.0, pltpu.roll(u, 1, axis=0))
    up = jnp.where(pos == period - 1, 0.0, pltpu.roll(u, tm - 1, axis=0))
    cv = _conv_taps(cw_ref, cb_ref, um, u, up)
    y = jnp.dot((gate_b * cv).astype(BF16), wout_ref[...], preferred_element_type=F32)
    o_ref[...] = x + g1_ref[...] * y


def _conv_rows_kernel(n_ctx_tiles, tiles_per_batch, xp_ref, x_ref, xn_ref, gain_ref, sh_ref,
                      sc_ref, g1_ref, win_ref, cw_ref, cb_ref, wout_ref, o_ref, h_ref):
    x = x_ref[...]
    tm, d = x.shape
    gain, sh, sc = gain_ref[...], sh_ref[...], sc_ref[...]
    h_ref[0:GRID_W, :] = _rms_mod(xp_ref[...], gain, sh, sc).astype(BF16)
    h_ref[GRID_W:GRID_W + tm, :] = _rms_mod(x, gain, sh, sc).astype(BF16)
    h_ref[GRID_W + tm:, :] = _rms_mod(xn_ref[...], gain, sh, sc).astype(BF16)
    p = jnp.dot(h_ref[...], win_ref[:, d:], preferred_element_type=F32)
    u = p[:, :d] * p[:, d:]
    gate_b = jnp.dot(h_ref[GRID_W:GRID_W + tm, :], win_ref[:, :d], preferred_element_type=F32)
    i = pl.program_id(0)
    jl = (i - n_ctx_tiles) % tiles_per_batch
    is_lat = i >= n_ctx_tiles
    has_prev = jnp.logical_and(is_lat, jl != 0)
    has_next = jnp.logical_and(is_lat, jl != tiles_per_batch - 1)
    row = lax.broadcasted_iota(jnp.int32, (tm, 1), 0)
    um = jnp.where(jnp.logical_or(row >= GRID_W, has_prev), u[0:tm], 0.0)
    up = jnp.where(jnp.logical_or(row < tm - GRID_W, has_next), u[2 * GRID_W:], 0.0)
    cv = _conv_taps(cw_ref, cb_ref, um, u[GRID_W:GRID_W + tm], up)
    y = jnp.dot((gate_b * cv).astype(BF16), wout_ref[...], preferred_element_type=F32)
    o_ref[...] = x + g1_ref[...] * y


def _conv_mixer(rows, x2, norm1, mods, w_in, cw, cb, w_out, layer, j, along_rows):
    d = rows.d
    tm = TM_CONV
    row = rows.mod_row(tm)
    n_tiles = rows.n_rows // tm
    n_ctx_tiles = rows.n_ctx // tm
    common = [
        _layer_vec_spec(d, layer),
        _mod_block(d, layer, 0, row),
        _mod_block(d, layer, 1, row),
        _mod_block(d, layer, 2, row),
        _const_spec((None, d, 3 * d), lambda i: (j, 0, 0)),
        pl.BlockSpec((None, 3, d), lambda i: (j, 0, 0)),
        _layer_vec_spec(d, j),
        _const_spec((None, d, d), lambda i: (j, 0, 0)),
    ]
    args = (norm1, mods, mods, mods, w_in, cw, cb, w_out)
    if along_rows:
        sub = tm // GRID_W
        last = rows.n_rows // GRID_W - 1
        kern = functools.partial(_conv_rows_kernel, n_ctx_tiles, rows.seq // tm)
        in_specs = [pl.BlockSpec((GRID_W, d), lambda i: (jnp.maximum(i * sub - 1, 0), 0)),
                    pl.BlockSpec((tm, d), lambda i: (i, 0)),
                    pl.BlockSpec((GRID_W, d), lambda i: (jnp.minimum((i + 1) * sub, last), 0))]
        args = (x2, x2, x2) + args
        scratch = [pltpu.VMEM((tm + 2 * GRID_W, d), BF16)]
        name = "conv_mixer_rows"
    else:
        kern = functools.partial(_conv_seq_kernel, n_ctx_tiles, rows.ctx_len)
        in_specs = [pl.BlockSpec((tm, d), lambda i: (i, 0))]
        args = (x2,) + args
        scratch = []
        name = "conv_mixer_seq"
    return pl.pallas_call(
        kern,
        grid=(n_tiles,),
        in_specs=in_specs + common,
        out_specs=pl.BlockSpec((tm, d), lambda i: (i, 0)),
        out_shape=jax.ShapeDtypeStruct((rows.n_rows, d), F32),
        scratch_shapes=scratch,
        compiler_params=_cparams(("parallel",)),
        name=name,
    )(*args)


def _mlp_residual(x_ref, gain_ref, sh_ref, sc_ref, g2_ref, w1_ref, w2_ref):
    x = x_ref[...]
    h = _rms_mod(x, gain_ref[...], sh_ref[...], sc_ref[...]).astype(BF16)
    a = jnp.dot(h, w1_ref[...], preferred_element_type=F32)
    a = jnp.square(jnp.maximum(a, 0.0)).astype(BF16)
    y = jnp.dot(a, w2_ref[...], preferred_element_type=F32)
    return x + g2_ref[...] * y


def _mlp_kernel(x_ref, gain_ref, sh_ref, sc_ref, g2_ref, w1_ref, w2_ref, o_ref):
    o_ref[...] = _mlp_residual(x_ref, gain_ref, sh_ref, sc_ref, g2_ref, w1_ref, w2_ref)


def _mlp_final_kernel(x_ref, gain_ref, sh_ref, sc_ref, g2_ref, w1_ref, w2_ref, gf_ref, o_ref):
    x = _mlp_residual(x_ref, gain_ref, sh_ref, sc_ref, g2_ref, w1_ref, w2_ref)
    o_ref[...] = x * lax.rsqrt(jnp.mean(x * x, axis=-1, keepdims=True) + EPS) * gf_ref[...]


def _mlp(rows, x2, norm2, mods, w1, w2, layer, norm_f=None):
    d = rows.d
    d_ff = w1.shape[-1]
    tm = TM_MLP
    final = norm_f is not None
    first = rows.n_ctx // tm if final else 0
    n_out = rows.n_lat if final else rows.n_rows
    row = lambda i: rows.mod_row(tm)(i + first)
    in_specs = [
        pl.BlockSpec((tm, d), lambda i: (i + first, 0)),
        _layer_vec_spec(d, layer),
        _mod_block(d, layer, 3, row),
        _mod_block(d, layer, 4, row),
        _mod_block(d, layer, 5, row),
        _const_spec((None, d, d_ff), lambda i: (layer, 0, 0)),
        _const_spec((None, d_ff, d), lambda i: (layer, 0, 0)),
    ]
    args = (x2, norm2, mods, mods, mods, w1, w2)
    if final:
        in_specs.append(pl.BlockSpec((1, d), lambda i: (0, 0)))
        args += (norm_f.reshape(1, d),)
    return pl.pallas_call(
        _mlp_final_kernel if final else _mlp_kernel,
        grid=(n_out // tm,),
        in_specs=in_specs,
        out_specs=pl.BlockSpec((tm, d), lambda i: (i, 0)),
        out_shape=jax.ShapeDtypeStruct((n_out, d), F32),
        compiler_params=_cparams(("parallel",)),
        name="mlp_final" if final else "mlp",
    )(*args)


def kernel(x, c, ctx, c_ctx, ada_w, ada_b, norm1, norm2, norm_f, mlp_w1, mlp_w2, hgrn_w_in, hgrn_lb,
           hgrn_gnorm, hgrn_w_out, conv_w_in, conv_w, conv_b, conv_w_out):
    bsz, seq, d = x.shape
    ctx_len = ctx.shape[1]
    depth = ada_w.shape[0]
    assert bsz <= CTX_MOD_ROW and d % HEAD_DIM == 0
    assert seq % max(TM_MLP, TM_CONV, T_SCAN) == 0 and ctx_len % T_SCAN == 0
    assert (bsz * ctx_len) % max(TM_MLP, TM_CONV) == 0
    assert ctx_len & (ctx_len - 1) == 0 and TM_CONV % ctx_len == 0
    rows = _Rows(bsz, seq, ctx_len, d)

    c8 = jnp.concatenate([c, c_ctx[None, :], jnp.zeros((MOD_ROWS - bsz - 1, d), F32)], axis=0)
    mods = _ada_table(c8, ada_w, ada_b).reshape(depth * MOD_ROWS * N_MODS, 1, d)
    lb = _lower_bounds(hgrn_lb).reshape(2, -1, 1, d)

    norm1 = norm1.reshape(depth, 1, d)
    norm2 = norm2.reshape(depth, 1, d)
    gnorm = hgrn_gnorm.reshape(-1, 1, d)
    conv_b = conv_b.reshape(-1, 1, d)
    w1, w2 = mlp_w1.astype(BF16), mlp_w2.astype(BF16)
    hw_in, hw_out = _prep_weight_pieces(hgrn_w_in), hgrn_w_out.astype(BF16)
    cw_in, cw_out = conv_w_in.astype(BF16), conv_w_out.astype(BF16)

    xs = (ctx.reshape(bsz * ctx_len, d), x.reshape(bsz * seq, d))
    for i in range(depth):
        j = i // N_MIXERS
        if i % N_MIXERS == 0:
            qsf, ksf, qsb, ksb, v, gate, dec = _hgrn_prep(rows, xs, norm1, mods, hw_in, lb, i, j)
            o_bwd = _hgrn_scan_bwd(rows, qsb, ksb, v, dec)
            x2 = _hgrn_scan_fwd(rows, qsf, ksf, v, dec, gate, o_bwd, xs, gnorm, mods, hw_out, i, j)
        else:
            x2 = _conv_mixer(rows, x2, norm1, mods, cw_in, conv_w, conv_b, cw_out, i, j,
                             along_rows=(j % 2 == 1))
        x2 = _mlp(rows, x2, norm2, mods, w1, w2, i, norm_f if i == depth - 1 else None)
        xs = (x2,)
    return x2.reshape(bsz, seq, d)
```

```python
import functools

import jax
import jax.numpy as jnp
from jax import lax
from jax.experimental import pallas as pl
from jax.experimental.pallas import tpu as pltpu

F32 = jnp.float32
BF16 = jnp.bfloat16

GRID_W = 64
HEAD_DIM = 128
CHUNK = 64
EPS = 1e-6
N_MIXERS = 2
CTX_MOD_ROW = 4
MOD_ROWS = 8
N_MODS = 6
DEC_ROWS = 8

VMEM_LIMIT = 56 * 1024 * 1024

TM_PREP = 512
PREP_PIECES = 4
PK_QS, PK_KS, PK_V, PK_GATE, PK_BLOCKS = 0, 1, 4, 5, 6
TM_MLP = 1024
FF_CHUNK = 1024
TM_CONV = 1024
T_SCAN = 256
TN_ADA = 2048


def _cparams(sem):
    return pltpu.CompilerParams(dimension_semantics=sem, vmem_limit_bytes=VMEM_LIMIT)


def _const_spec(shape, index_map):
    return pl.BlockSpec(shape, index_map, pipeline_mode=pl.Buffered(1))


def _silu(t):
    return t * jax.nn.sigmoid(t)


def _rms_mod(x, gain, shift, scale):
    y = x * lax.rsqrt(jnp.mean(x * x, axis=-1, keepdims=True) + EPS) * gain
    return y * (1.0 + scale) + shift


def _ada_kernel(c_ref, w_ref, b_ref, o_ref):
    s = _silu(c_ref[...]).astype(BF16)
    o_ref[...] = jnp.dot(s, w_ref[...].astype(BF16), preferred_element_type=F32) + b_ref[...]


def _ada_table(c8, ada_w, ada_b):
    depth, d, n = ada_w.shape
    return pl.pallas_call(
        _ada_kernel,
        grid=(depth, n // TN_ADA),
        in_specs=[
            pl.BlockSpec((MOD_ROWS, d), lambda l, j: (0, 0)),
            pl.BlockSpec((None, d, TN_ADA), lambda l, j: (l, 0, j)),
            pl.BlockSpec((None, 1, TN_ADA), lambda l, j: (l, 0, j)),
        ],
        out_specs=pl.BlockSpec((None, MOD_ROWS, TN_ADA), lambda l, j: (l, 0, j)),
        out_shape=jax.ShapeDtypeStruct((depth, MOD_ROWS, n), F32),
        compiler_params=_cparams(("parallel", "parallel")),
        name="ada_table",
    )(c8, ada_w, ada_b.reshape(depth, 1, n))


def _lb_kernel(x_ref, o_ref):
    n_rec = x_ref.shape[1]
    for d in range(x_ref.shape[0]):
        rows = [x_ref[d, j:j + 1, :] for j in range(n_rec)]
        m = functools.reduce(jnp.maximum, rows)
        e = [jnp.exp(r - m) for r in rows]
        tot = functools.reduce(lambda a, b: a + b, e)
        acc = None
        for j in range(n_rec):
            p = e[j] / tot
            acc = p if acc is None else acc + p
            o_ref[d, j:j + 1, :] = acc - e[0] / tot


def _lower_bounds(hgrn_lb):
    return pl.pallas_call(
        _lb_kernel,
        out_shape=jax.ShapeDtypeStruct(hgrn_lb.shape, F32),
        name="hgrn_lower_bounds",
    )(hgrn_lb)


class _Rows:
    def __init__(self, bsz, seq, ctx_len, d):
        self.bsz, self.seq, self.ctx_len, self.d = bsz, seq, ctx_len, d
        self.n_ctx = bsz * ctx_len
        self.n_lat = bsz * seq
        self.n_rows = self.n_ctx + self.n_lat

    def mod_row(self, tm):
        n_ctx_tiles = self.n_ctx // tm
        per_batch = self.seq // tm
        return lambda i: jnp.where(i < n_ctx_tiles, CTX_MOD_ROW, (i - n_ctx_tiles) // per_batch)


def _mod_block(d, layer, k, row_of_grid):
    return pl.BlockSpec((None, 1, d),
                        lambda *g: ((layer * MOD_ROWS + row_of_grid(*g)) * N_MODS + k, 0, 0))


def _layer_vec_spec(d, layer):
    return pl.BlockSpec((None, 1, d), lambda *g: (layer, 0, 0))


def _cumsum_rows(g, reverse):
    n = g.shape[0]
    row = lax.broadcasted_iota(jnp.int32, g.shape, 0)
    s = 1
    while s < n:
        if reverse:
            g = g + jnp.where(row < n - s, pltpu.roll(g, n - s, axis=0), 0.0)
        else:
            g = g + jnp.where(row >= s, pltpu.roll(g, s, axis=0), 0.0)
        s *= 2
    return g


def _split_sources(n_src, refs):
    return refs[:n_src], refs[n_src:]


def _load_rows(x_refs, is_ctx):
    if len(x_refs) == 1:
        return x_refs[0][...]
    return jnp.where(is_ctx, x_refs[0][...], x_refs[1][...])


def _prep_kernel(n_src, ctx_tiles, *refs):
    x_refs, refs = _split_sources(n_src, refs)
    (gain_ref, sh_ref, sc_ref, w_ref, lbf_ref, lbb_ref,
     pk_ref, dec_ref, za_ref, zb_ref, h_ref) = refs
    tm, d = h_ref.shape
    n_chunks = tm // CHUNK
    n_units = za_ref.shape[0]
    wpiece = za_ref.shape[-1]
    n_parts = n_units * wpiece // d
    ws = d // n_units
    step = pl.program_id(0)
    lb_refs = (lbf_ref, lbb_ref)
    assert n_parts == 5 and wpiece == n_parts * ws

    @pl.when(step == 0)
    def _():
        zb_ref[...] = jnp.zeros(zb_ref.shape, F32)

    def body(zw_ref, zr_ref):
        x = _load_rows(x_refs, step < ctx_tiles)
        h_ref[...] = _rms_mod(x, gain_ref[...], sh_ref[...], sc_ref[...]).astype(BF16)
        dec_ref[:, 4:, :] = jnp.zeros((n_chunks, DEC_ROWS - 4, d), F32)
        for ci in range(n_chunks):
            if ci % (n_chunks // n_units) == 0:
                it = ci // (n_chunks // n_units)
                zw_ref[it] = jnp.dot(h_ref[...], w_ref[it], preferred_element_type=F32)
            rows = slice(ci * CHUNK, (ci + 1) * CHUNK)
            for j in range(n_units):
                cols = slice(j * ws, (j + 1) * ws)

                def part(p):
                    piece, k = divmod(p * n_units + j, n_parts)
                    return zr_ref[piece, rows, k * ws:(k + 1) * ws]

                def put(block, val):
                    pk_ref[rows, block * d + j * ws:block * d + (j + 1) * ws] = val.astype(BF16)

                put(PK_V, part(2))
                put(PK_GATE, _silu(part(4)))
                q = _silu(part(3))
                for dirn, lb_ref in enumerate(lb_refs):
                    lb = lb_ref[:, cols]
                    f = lb + (1.0 - lb) * jax.nn.sigmoid(part(dirn))
                    cum = _cumsum_rows(jnp.log(f), dirn == 1)
                    if dirn == 1:
                        ref, last = cum[CHUNK // 2:CHUNK // 2 + 1], cum[0:1]
                    else:
                        ref, last = cum[CHUNK // 2 - 1:CHUNK // 2], cum[CHUNK - 1:CHUNK]
                    put(PK_QS + 2 * dirn, q * jnp.exp(cum - ref))
                    put(PK_KS + 2 * dirn, (1.0 - f) * jnp.exp(ref - cum))
                    dec_ref[ci, 2 * dirn:2 * dirn + 1, cols] = jnp.exp(ref)
                    dec_ref[ci, 2 * dirn + 1:2 * dirn + 2, cols] = jnp.exp(last - ref)

    @pl.when(step % 2 == 0)
    def _():
        body(za_ref, zb_ref)

    @pl.when(step % 2 == 1)
    def _():
        body(zb_ref, za_ref)


def _source_specs(rows, xs, tm, tile_of_grid):
    d = rows.d
    if len(xs) == 1:
        return [pl.BlockSpec((tm, d), lambda *g: (tile_of_grid(*g), 0))]
    n_ctx_tiles = rows.n_ctx // tm
    return [pl.BlockSpec((tm, d), lambda *g: (jnp.minimum(tile_of_grid(*g), n_ctx_tiles - 1), 0)),
            pl.BlockSpec((tm, d), lambda *g: (jnp.maximum(tile_of_grid(*g) - n_ctx_tiles, 0), 0))]


def _prep_weight_pieces(w_in):
    n_layers, d, n = w_in.shape
    w = w_in.astype(BF16).reshape(n_layers, d, PREP_PIECES, n // PREP_PIECES)
    return w.transpose(0, 2, 1, 3)


def _hgrn_prep(rows, xs, norm1, mods, w_pieces, lb, layer, j):
    d = rows.d
    tm = TM_PREP
    n_pieces, _, piece_cols = w_pieces.shape[1:]
    n_tiles = rows.n_rows // tm
    cur = lambda s: jnp.minimum(s, n_tiles - 1)
    prev = lambda s: jnp.maximum(s - 1, 0)
    row = lambda s: rows.mod_row(tm)(cur(s))
    z_shape = pltpu.VMEM((n_pieces, tm, piece_cols), F32)
    return pl.pallas_call(
        functools.partial(_prep_kernel, len(xs), rows.n_ctx // tm),
        grid=(n_tiles + 1,),
        in_specs=_source_specs(rows, xs, tm, cur) + [
            _layer_vec_spec(d, layer),
            _mod_block(d, layer, 0, row),
            _mod_block(d, layer, 1, row),
            _const_spec((None, n_pieces, d, piece_cols), lambda s: (j, 0, 0, 0)),
            pl.BlockSpec((None, None, 1, d), lambda s: (0, j, 0, 0)),
            pl.BlockSpec((None, None, 1, d), lambda s: (1, j, 0, 0)),
        ],
        out_specs=[pl.BlockSpec((tm, PK_BLOCKS * d), lambda s: (prev(s), 0)),
                   pl.BlockSpec((tm // CHUNK, DEC_ROWS, d), lambda s: (prev(s), 0, 0))],
        out_shape=[jax.ShapeDtypeStruct((rows.n_rows, PK_BLOCKS * d), BF16),
                   jax.ShapeDtypeStruct((rows.n_rows // CHUNK, DEC_ROWS, d), F32)],
        scratch_shapes=[z_shape, z_shape, pltpu.VMEM((tm, d), BF16)],
        compiler_params=_cparams(("arbitrary",)),
        name="hgrn_prep",
    )(*xs, norm1, mods, mods, w_pieces, lb, lb)


def _gla_tile(qs_ref, ks_ref, v_ref, dec_ref, s_ref, dirn, emit):
    t_rows, d = qs_ref.shape
    reverse = dirn == 1
    n_chunks = t_rows // CHUNK
    r = lax.broadcasted_iota(jnp.int32, (CHUNK, CHUNK), 0)
    c = lax.broadcasted_iota(jnp.int32, (CHUNK, CHUNK), 1)
    mask = (c >= r) if reverse else (c <= r)
    nt = (((1,), (1,)), ((), ()))
    tn = (((0,), (0,)), ((), ()))
    order = range(n_chunks - 1, -1, -1) if reverse else range(n_chunks)
    heads = range(d // HEAD_DIM)
    hsl = lambda h: slice(h * HEAD_DIM, (h + 1) * HEAD_DIM)
    rsl = lambda ci: slice(ci * CHUNK, (ci + 1) * CHUNK)
    p, u = {}, {}
    for h in heads:
        for ci in order:
            sc = lax.dot_general(qs_ref[rsl(ci), hsl(h)], ks_ref[rsl(ci), hsl(h)], nt,
                                 preferred_element_type=F32)
            p[h, ci] = jnp.where(mask, sc, 0.0).astype(BF16)
    for h in heads:
        for ci in order:
            u[h, ci] = lax.dot_general(v_ref[rsl(ci), hsl(h)], ks_ref[rsl(ci), hsl(h)], tn,
                                       preferred_element_type=F32)
    for h in heads:
        st = s_ref[h]
        for ci in order:
            e_ref = dec_ref[ci, 2 * dirn:2 * dirn + 1, hsl(h)]
            e_lr = dec_ref[ci, 2 * dirn + 1:2 * dirn + 2, hsl(h)]
            o = lax.dot_general(qs_ref[rsl(ci), hsl(h)], (st * e_ref).astype(BF16), nt,
                                preferred_element_type=F32)
            o = o + jnp.dot(p[h, ci], v_ref[rsl(ci), hsl(h)], preferred_element_type=F32)
            st = st * (e_ref * e_lr) + u[h, ci] * e_lr
            emit(rsl(ci), hsl(h), o)
        s_ref[h] = st


def _scan_bwd_kernel(qs_ref, ks_ref, v_ref, dec_ref, o_ref, s_ref):
    @pl.when(pl.program_id(1) == 0)
    def _():
        s_ref[...] = jnp.zeros_like(s_ref)

    def emit(rows, hs, o):
        o_ref[rows, hs] = o

    _gla_tile(qs_ref, ks_ref, v_ref, dec_ref, s_ref, 1, emit)


def _scan_fwd_kernel(n_src, ctx_steps, *refs):
    x_refs, refs = _split_sources(n_src, refs)
    (qs_ref, ks_ref, v_ref, dec_ref, gate_ref, ob_ref, gn_ref, g1_ref, wout_ref,
     o_ref, s_ref, y_ref) = refs

    @pl.when(pl.program_id(1) == 0)
    def _():
        s_ref[...] = jnp.zeros_like(s_ref)

    def emit(rows, hs, o):
        o = o + ob_ref[rows, hs]
        o = o * lax.rsqrt(jnp.mean(o * o, axis=-1, keepdims=True) + EPS)
        y_ref[rows, hs] = (o * gn_ref[:, hs] * gate_ref[rows, hs].astype(F32)).astype(BF16)

    _gla_tile(qs_ref, ks_ref, v_ref, dec_ref, s_ref, 0, emit)
    y = jnp.dot(y_ref[...], wout_ref[...], preferred_element_type=F32)
    o_ref[...] = _load_rows(x_refs, pl.program_id(1) < ctx_steps) + g1_ref[...] * y


def _scan_tiles(rows, reverse):
    t = T_SCAN
    nct = rows.ctx_len // t
    ntl = rows.seq // t
    ctx_tiles = rows.n_ctx // t

    def tile(b, s):
        if reverse:
            return jnp.where(s < nct, b * nct + (nct - 1 - s), ctx_tiles + b * ntl + (ntl - 1 - (s - nct)))
        return jnp.where(s < nct, b * nct + s, ctx_tiles + b * ntl + (s - nct))

    mod_row = lambda b, s: jnp.where(s < nct, CTX_MOD_ROW, b)
    return tile, mod_row, nct + ntl


def _hgrn_scan_bwd(rows, packed, dec):
    d = rows.d
    t = T_SCAN
    tile, _, steps = _scan_tiles(rows, True)
    act = pl.BlockSpec((t, d), lambda b, s: (tile(b, s), 0))
    pk = lambda k: pl.BlockSpec((t, d), lambda b, s: (tile(b, s), k))
    return pl.pallas_call(
        _scan_bwd_kernel,
        grid=(rows.bsz, steps),
        in_specs=[pk(PK_QS + 2), pk(PK_KS + 2), pk(PK_V),
                  pl.BlockSpec((t // CHUNK, DEC_ROWS, d), lambda b, s: (tile(b, s), 0, 0))],
        out_specs=act,
        out_shape=jax.ShapeDtypeStruct((rows.n_rows, d), F32),
        scratch_shapes=[pltpu.VMEM((d // HEAD_DIM, HEAD_DIM, HEAD_DIM), F32)],
        compiler_params=_cparams(("arbitrary", "arbitrary")),
        name="hgrn_scan_bwd",
    )(packed, packed, packed, dec)


def _hgrn_scan_fwd(rows, packed, dec, o_bwd, xs, gnorm, mods, w_out, layer, j):
    d = rows.d
    t = T_SCAN
    tile, mod_row, steps = _scan_tiles(rows, False)
    act = pl.BlockSpec((t, d), lambda b, s: (tile(b, s), 0))
    pk = lambda k: pl.BlockSpec((t, d), lambda b, s: (tile(b, s), k))
    return pl.pallas_call(
        functools.partial(_scan_fwd_kernel, len(xs), rows.ctx_len // t),
        grid=(rows.bsz, steps),
        in_specs=_source_specs(rows, xs, t, tile) + [
                  pk(PK_QS), pk(PK_KS), pk(PK_V),
                  pl.BlockSpec((t // CHUNK, DEC_ROWS, d), lambda b, s: (tile(b, s), 0, 0)),
                  pk(PK_GATE), act,
                  _layer_vec_spec(d, j),
                  _mod_block(d, layer, 2, mod_row),
                  _const_spec((None, d, d), lambda b, s: (j, 0, 0))],
        out_specs=act,
        out_shape=jax.ShapeDtypeStruct((rows.n_rows, d), F32),
        scratch_shapes=[pltpu.VMEM((d // HEAD_DIM, HEAD_DIM, HEAD_DIM), F32),
                        pltpu.VMEM((t, d), BF16)],
        compiler_params=_cparams(("arbitrary", "arbitrary")),
        name="hgrn_scan_fwd",
    )(*xs, packed, packed, packed, dec, packed, o_bwd, gnorm, mods, w_out)


def _conv_taps(cw_ref, cb_ref, um, u, up):
    return cb_ref[...] + cw_ref[0:1, :] * um + cw_ref[1:2, :] * u + cw_ref[2:3, :] * up


def _conv_seq_kernel(n_ctx_tiles, ctx_len, x_ref, gain_ref, sh_ref, sc_ref, g1_ref, win_ref,
                     cw_ref, cb_ref, wout_ref, o_ref):
    x = x_ref[...]
    tm, d = x.shape
    h = _rms_mod(x, gain_ref[...], sh_ref[...], sc_ref[...]).astype(BF16)
    p = jnp.dot(h, win_ref[...], preferred_element_type=F32)
    gate_b, u = p[:, :d], p[:, d:2 * d] * p[:, 2 * d:]
    period = jnp.where(pl.program_id(0) < n_ctx_tiles, ctx_len, GRID_W)
    pos = lax.broadcasted_iota(jnp.int32, (tm, 1), 0) & (period - 1)
    um = jnp.where(pos == 0, 0.0, pltpu.roll(u, 1, axis=0))
    up = jnp.where(pos == period - 1, 0.0, pltpu.roll(u, tm - 1, axis=0))
    cv = _conv_taps(cw_ref, cb_ref, um, u, up)
    y = jnp.dot((gate_b * cv).astype(BF16), wout_ref[...], preferred_element_type=F32)
    o_ref[...] = x + g1_ref[...] * y


def _conv_rows_kernel(n_ctx_tiles, tiles_per_batch, xp_ref, x_ref, xn_ref, gain_ref, sh_ref,
                      sc_ref, g1_ref, win_ref, cw_ref, cb_ref, wout_ref, o_ref, h_ref):
    x = x_ref[...]
    tm, d = x.shape
    gain, sh, sc = gain_ref[...], sh_ref[...], sc_ref[...]
    h_ref[0:GRID_W, :] = _rms_mod(xp_ref[...], gain, sh, sc).astype(BF16)
    h_ref[GRID_W:GRID_W + tm, :] = _rms_mod(x, gain, sh, sc).astype(BF16)
    h_ref[GRID_W + tm:, :] = _rms_mod(xn_ref[...], gain, sh, sc).astype(BF16)
    p = jnp.dot(h_ref[...], win_ref[:, d:], preferred_element_type=F32)
    u = p[:, :d] * p[:, d:]
    gate_b = jnp.dot(h_ref[GRID_W:GRID_W + tm, :], win_ref[:, :d], preferred_element_type=F32)
    i = pl.program_id(0)
    jl = (i - n_ctx_tiles) % tiles_per_batch
    is_lat = i >= n_ctx_tiles
    has_prev = jnp.logical_and(is_lat, jl != 0)
    has_next = jnp.logical_and(is_lat, jl != tiles_per_batch - 1)
    row = lax.broadcasted_iota(jnp.int32, (tm, 1), 0)
    um = jnp.where(jnp.logical_or(row >= GRID_W, has_prev), u[0:tm], 0.0)
    up = jnp.where(jnp.logical_or(row < tm - GRID_W, has_next), u[2 * GRID_W:], 0.0)
    cv = _conv_taps(cw_ref, cb_ref, um, u[GRID_W:GRID_W + tm], up)
    y = jnp.dot((gate_b * cv).astype(BF16), wout_ref[...], preferred_element_type=F32)
    o_ref[...] = x + g1_ref[...] * y


def _conv_mixer(rows, x2, norm1, mods, w_in, cw, cb, w_out, layer, j, along_rows):
    d = rows.d
    tm = TM_CONV
    row = rows.mod_row(tm)
    n_tiles = rows.n_rows // tm
    n_ctx_tiles = rows.n_ctx // tm
    common = [
        _layer_vec_spec(d, layer),
        _mod_block(d, layer, 0, row),
        _mod_block(d, layer, 1, row),
        _mod_block(d, layer, 2, row),
        _const_spec((None, d, 3 * d), lambda i: (j, 0, 0)),
        pl.BlockSpec((None, 3, d), lambda i: (j, 0, 0)),
        _layer_vec_spec(d, j),
        _const_spec((None, d, d), lambda i: (j, 0, 0)),
    ]
    args = (norm1, mods, mods, mods, w_in, cw, cb, w_out)
    if along_rows:
        sub = tm // GRID_W
        last = rows.n_rows // GRID_W - 1
        kern = functools.partial(_conv_rows_kernel, n_ctx_tiles, rows.seq // tm)
        in_specs = [pl.BlockSpec((GRID_W, d), lambda i: (jnp.maximum(i * sub - 1, 0), 0)),
                    pl.BlockSpec((tm, d), lambda i: (i, 0)),
                    pl.BlockSpec((GRID_W, d), lambda i: (jnp.minimum((i + 1) * sub, last), 0))]
        args = (x2, x2, x2) + args
        scratch = [pltpu.VMEM((tm + 2 * GRID_W, d), BF16)]
        name = "conv_mixer_rows"
    else:
        kern = functools.partial(_conv_seq_kernel, n_ctx_tiles, rows.ctx_len)
        in_specs = [pl.BlockSpec((tm, d), lambda i: (i, 0))]
        args = (x2,) + args
        scratch = []
        name = "conv_mixer_seq"
    return pl.pallas_call(
        kern,
        grid=(n_tiles,),
        in_specs=in_specs + common,
        out_specs=pl.BlockSpec((tm, d), lambda i: (i, 0)),
        out_shape=jax.ShapeDtypeStruct((rows.n_rows, d), F32),
        scratch_shapes=scratch,
        compiler_params=_cparams(("parallel",)),
        name=name,
    )(*args)


def _mlp_residual(x_ref, gain_ref, sh_ref, sc_ref, g2_ref, w1_ref, w2_ref):
    x = x_ref[...]
    h = _rms_mod(x, gain_ref[...], sh_ref[...], sc_ref[...]).astype(BF16)
    d_ff = w1_ref.shape[-1]
    y = None
    for lo in range(0, d_ff, FF_CHUNK):
        a = jnp.dot(h, w1_ref[:, lo:lo + FF_CHUNK], preferred_element_type=F32)
        a = jnp.square(jnp.maximum(a, 0.0)).astype(BF16)
        part = jnp.dot(a, w2_ref[lo:lo + FF_CHUNK, :], preferred_element_type=F32)
        y = part if y is None else y + part
    return x + g2_ref[...] * y


def _mlp_kernel(x_ref, gain_ref, sh_ref, sc_ref, g2_ref, w1_ref, w2_ref, o_ref):
    o_ref[...] = _mlp_residual(x_ref, gain_ref, sh_ref, sc_ref, g2_ref, w1_ref, w2_ref)


def _mlp_final_kernel(x_ref, gain_ref, sh_ref, sc_ref, g2_ref, w1_ref, w2_ref, gf_ref, o_ref):
    x = _mlp_residual(x_ref, gain_ref, sh_ref, sc_ref, g2_ref, w1_ref, w2_ref)
    o_ref[...] = x * lax.rsqrt(jnp.mean(x * x, axis=-1, keepdims=True) + EPS) * gf_ref[...]


def _mlp(rows, x2, norm2, mods, w1, w2, layer, norm_f=None):
    d = rows.d
    d_ff = w1.shape[-1]
    tm = TM_MLP
    final = norm_f is not None
    first = rows.n_ctx // tm if final else 0
    n_out = rows.n_lat if final else rows.n_rows
    row = lambda i: rows.mod_row(tm)(i + first)
    in_specs = [
        pl.BlockSpec((tm, d), lambda i: (i + first, 0)),
        _layer_vec_spec(d, layer),
        _mod_block(d, layer, 3, row),
        _mod_block(d, layer, 4, row),
        _mod_block(d, layer, 5, row),
        _const_spec((None, d, d_ff), lambda i: (layer, 0, 0)),
        _const_spec((None, d_ff, d), lambda i: (layer, 0, 0)),
    ]
    args = (x2, norm2, mods, mods, mods, w1, w2)
    if final:
        in_specs.append(pl.BlockSpec((1, d), lambda i: (0, 0)))
        args += (norm_f.reshape(1, d),)
    return pl.pallas_call(
        _mlp_final_kernel if final else _mlp_kernel,
        grid=(n_out // tm,),
        in_specs=in_specs,
        out_specs=pl.BlockSpec((tm, d), lambda i: (i, 0)),
        out_shape=jax.ShapeDtypeStruct((n_out, d), F32),
        compiler_params=_cparams(("parallel",)),
        name="mlp_final" if final else "mlp",
    )(*args)


def kernel(x, c, ctx, c_ctx, ada_w, ada_b, norm1, norm2, norm_f, mlp_w1, mlp_w2, hgrn_w_in, hgrn_lb,
           hgrn_gnorm, hgrn_w_out, conv_w_in, conv_w, conv_b, conv_w_out):
    bsz, seq, d = x.shape
    ctx_len = ctx.shape[1]
    depth = ada_w.shape[0]
    assert bsz <= CTX_MOD_ROW and d % HEAD_DIM == 0
    assert seq % max(TM_MLP, TM_CONV, T_SCAN) == 0 and ctx_len % T_SCAN == 0
    assert (bsz * ctx_len) % max(TM_MLP, TM_CONV) == 0
    assert ctx_len & (ctx_len - 1) == 0 and TM_CONV % ctx_len == 0
    rows = _Rows(bsz, seq, ctx_len, d)

    c8 = jnp.concatenate([c, c_ctx[None, :], jnp.zeros((MOD_ROWS - bsz - 1, d), F32)], axis=0)
    mods = _ada_table(c8, ada_w, ada_b).reshape(depth * MOD_ROWS * N_MODS, 1, d)
    lb = _lower_bounds(hgrn_lb).reshape(2, -1, 1, d)

    norm1 = norm1.reshape(depth, 1, d)
    norm2 = norm2.reshape(depth, 1, d)
    gnorm = hgrn_gnorm.reshape(-1, 1, d)
    conv_b = conv_b.reshape(-1, 1, d)
    w1, w2 = mlp_w1.astype(BF16), mlp_w2.astype(BF16)
    hw_in, hw_out = _prep_weight_pieces(hgrn_w_in), hgrn_w_out.astype(BF16)
    cw_in, cw_out = conv_w_in.astype(BF16), conv_w_out.astype(BF16)

    xs = (ctx.reshape(bsz * ctx_len, d), x.reshape(bsz * seq, d))
    for i in range(depth):
        j = i // N_MIXERS
        if i % N_MIXERS == 0:
            packed, dec = _hgrn_prep(rows, xs, norm1, mods, hw_in, lb, i, j)
            o_bwd = _hgrn_scan_bwd(rows, packed, dec)
            x2 = _hgrn_scan_fwd(rows, packed, dec, o_bwd, xs, gnorm, mods, hw_out, i, j)
        else:
            x2 = _conv_mixer(rows, x2, norm1, mods, cw_in, conv_w, conv_b, cw_out, i, j,
                             along_rows=(j % 2 == 1))
        x2 = _mlp(rows, x2, norm2, mods, w1, w2, i, norm_f if i == depth - 1 else None)
        xs = (x2,)
    return x2.reshape(bsz, seq, d)
```

```python
import functools

import jax
import jax.numpy as jnp
from jax import lax
from jax.experimental import pallas as pl
from jax.experimental.pallas import tpu as pltpu

F32 = jnp.float32
BF16 = jnp.bfloat16

GRID_W = 64
HEAD_DIM = 128
CHUNK = 64
EPS = 1e-6
N_MIXERS = 2
CTX_MOD_ROW = 4
MOD_ROWS = 8
N_MODS = 6
DEC_ROWS = 8

VMEM_LIMIT = 56 * 1024 * 1024

TM_PREP = 512
PREP_PIECES = 4
PK_QS, PK_KS, PK_V, PK_GATE, PK_BLOCKS = 0, 1, 4, 5, 6
TM_MLP = 1024
FF_CHUNK = 1024
TM_CONV = 1024
T_SCAN = 256
TN_ADA = 2048


def _cparams(sem):
    return pltpu.CompilerParams(dimension_semantics=sem, vmem_limit_bytes=VMEM_LIMIT)


def _const_spec(shape, index_map):
    return pl.BlockSpec(shape, index_map, pipeline_mode=pl.Buffered(1))


def _silu(t):
    return t * jax.nn.sigmoid(t)


def _rms_mod(x, gain, shift, scale):
    y = x * lax.rsqrt(jnp.mean(x * x, axis=-1, keepdims=True) + EPS) * gain
    return y * (1.0 + scale) + shift


def _ada_kernel(c_ref, w_ref, b_ref, o_ref):
    s = _silu(c_ref[...]).astype(BF16)
    o_ref[...] = jnp.dot(s, w_ref[...].astype(BF16), preferred_element_type=F32) + b_ref[...]


def _ada_table(c8, ada_w, ada_b):
    depth, d, n = ada_w.shape
    return pl.pallas_call(
        _ada_kernel,
        grid=(depth, n // TN_ADA),
        in_specs=[
            pl.BlockSpec((MOD_ROWS, d), lambda l, j: (0, 0)),
            pl.BlockSpec((None, d, TN_ADA), lambda l, j: (l, 0, j)),
            pl.BlockSpec((None, 1, TN_ADA), lambda l, j: (l, 0, j)),
        ],
        out_specs=pl.BlockSpec((None, MOD_ROWS, TN_ADA), lambda l, j: (l, 0, j)),
        out_shape=jax.ShapeDtypeStruct((depth, MOD_ROWS, n), F32),
        compiler_params=_cparams(("parallel", "parallel")),
        name="ada_table",
    )(c8, ada_w, ada_b.reshape(depth, 1, n))


def _lb_kernel(x_ref, o_ref):
    n_rec = x_ref.shape[1]
    for d in range(x_ref.shape[0]):
        rows = [x_ref[d, j:j + 1, :] for j in range(n_rec)]
        m = functools.reduce(jnp.maximum, rows)
        e = [jnp.exp(r - m) for r in rows]
        tot = functools.reduce(lambda a, b: a + b, e)
        acc = None
        for j in range(n_rec):
            p = e[j] / tot
            acc = p if acc is None else acc + p
            o_ref[d, j:j + 1, :] = acc - e[0] / tot


def _lower_bounds(hgrn_lb):
    return pl.pallas_call(
        _lb_kernel,
        out_shape=jax.ShapeDtypeStruct(hgrn_lb.shape, F32),
        name="hgrn_lower_bounds",
    )(hgrn_lb)


class _Rows:
    def __init__(self, bsz, seq, ctx_len, d):
        self.bsz, self.seq, self.ctx_len, self.d = bsz, seq, ctx_len, d
        self.n_ctx = bsz * ctx_len
        self.n_lat = bsz * seq
        self.n_rows = self.n_ctx + self.n_lat

    def mod_row(self, tm):
        n_ctx_tiles = self.n_ctx // tm
        per_batch = self.seq // tm
        return lambda i: jnp.where(i < n_ctx_tiles, CTX_MOD_ROW, (i - n_ctx_tiles) // per_batch)


def _mod_block(d, layer, k, row_of_grid):
    return pl.BlockSpec((None, 1, d),
                        lambda *g: ((layer * MOD_ROWS + row_of_grid(*g)) * N_MODS + k, 0, 0))


def _layer_vec_spec(d, layer):
    return pl.BlockSpec((None, 1, d), lambda *g: (layer, 0, 0))


def _cumsum_rows(g, reverse):
    n = g.shape[0]
    row = lax.broadcasted_iota(jnp.int32, g.shape, 0)
    s = 1
    while s < n:
        if reverse:
            g = g + jnp.where(row < n - s, pltpu.roll(g, n - s, axis=0), 0.0)
        else:
            g = g + jnp.where(row >= s, pltpu.roll(g, s, axis=0), 0.0)
        s *= 2
    return g


def _split_sources(n_src, refs):
    return refs[:n_src], refs[n_src:]


def _load_rows(x_refs, is_ctx):
    if len(x_refs) == 1:
        return x_refs[0][...]
    return jnp.where(is_ctx, x_refs[0][...], x_refs[1][...])


def _prep_kernel(n_src, ctx_tiles, *refs):
    x_refs, refs = _split_sources(n_src, refs)
    (gain_ref, sh_ref, sc_ref, w_ref, lbf_ref, lbb_ref,
     pk_ref, dec_ref, za_ref, zb_ref, h_ref) = refs
    tm, d = h_ref.shape
    n_chunks = tm // CHUNK
    n_units = za_ref.shape[0]
    wpiece = za_ref.shape[-1]
    n_parts = n_units * wpiece // d
    ws = d // n_units
    step = pl.program_id(0)
    lb_refs = (lbf_ref, lbb_ref)
    assert n_parts == 5 and wpiece == n_parts * ws

    @pl.when(step == 0)
    def _():
        zb_ref[...] = jnp.zeros(zb_ref.shape, F32)

    def body(zw_ref, zr_ref):
        x = _load_rows(x_refs, step < ctx_tiles)
        h_ref[...] = _rms_mod(x, gain_ref[...], sh_ref[...], sc_ref[...]).astype(BF16)
        dec_ref[:, 4:, :] = jnp.zeros((n_chunks, DEC_ROWS - 4, d), F32)
        for ci in range(n_chunks):
            if ci % (n_chunks // n_units) == 0:
                it = ci // (n_chunks // n_units)
                zw_ref[it] = jnp.dot(h_ref[...], w_ref[it], preferred_element_type=F32)
            rows = slice(ci * CHUNK, (ci + 1) * CHUNK)
            for j in range(n_units):
                cols = slice(j * ws, (j + 1) * ws)

                def part(p):
                    piece, k = divmod(p * n_units + j, n_parts)
                    return zr_ref[piece, rows, k * ws:(k + 1) * ws]

                def put(block, val):
                    pk_ref[rows, block * d + j * ws:block * d + (j + 1) * ws] = val.astype(BF16)

                put(PK_V, part(2))
                put(PK_GATE, _silu(part(4)))
                q = _silu(part(3))
                for dirn, lb_ref in enumerate(lb_refs):
                    lb = lb_ref[:, cols]
                    f = lb + (1.0 - lb) * jax.nn.sigmoid(part(dirn))
                    cum = _cumsum_rows(jnp.log(f), dirn == 1)
                    if dirn == 1:
                        ref, last = cum[CHUNK // 2:CHUNK // 2 + 1], cum[0:1]
                    else:
                        ref, last = cum[CHUNK // 2 - 1:CHUNK // 2], cum[CHUNK - 1:CHUNK]
                    put(PK_QS + 2 * dirn, q * jnp.exp(cum - ref))
                    put(PK_KS + 2 * dirn, (1.0 - f) * jnp.exp(ref - cum))
                    dec_ref[ci, 2 * dirn:2 * dirn + 1, cols] = jnp.exp(ref)
                    dec_ref[ci, 2 * dirn + 1:2 * dirn + 2, cols] = jnp.exp(last - ref)

    @pl.when(step % 2 == 0)
    def _():
        body(za_ref, zb_ref)

    @pl.when(step % 2 == 1)
    def _():
        body(zb_ref, za_ref)


def _source_specs(rows, xs, tm, tile_of_grid):
    d = rows.d
    if len(xs) == 1:
        return [pl.BlockSpec((tm, d), lambda *g: (tile_of_grid(*g), 0))]
    n_ctx_tiles = rows.n_ctx // tm
    return [pl.BlockSpec((tm, d), lambda *g: (jnp.minimum(tile_of_grid(*g), n_ctx_tiles - 1), 0)),
            pl.BlockSpec((tm, d), lambda *g: (jnp.maximum(tile_of_grid(*g) - n_ctx_tiles, 0), 0))]


def _cast_kernel(w_ref, o_ref):
    o_ref[...] = w_ref[...].astype(o_ref.dtype)


def _prep_weight_pieces(w_in):
    n_layers, d, n = w_in.shape
    wp = n // PREP_PIECES
    return pl.pallas_call(
        _cast_kernel,
        grid=(n_layers, PREP_PIECES),
        in_specs=[pl.BlockSpec((None, d, wp), lambda l, i: (l, 0, i))],
        out_specs=pl.BlockSpec((None, None, d, wp), lambda l, i: (l, i, 0, 0)),
        out_shape=jax.ShapeDtypeStruct((n_layers, PREP_PIECES, d, wp), BF16),
        compiler_params=_cparams(("parallel", "parallel")),
        name="prep_weight_pieces",
    )(w_in)


def _hgrn_prep(rows, xs, norm1, mods, w_pieces, lb, layer, j):
    d = rows.d
    tm = TM_PREP
    n_pieces, _, piece_cols = w_pieces.shape[1:]
    n_tiles = rows.n_rows // tm
    cur = lambda s: jnp.minimum(s, n_tiles - 1)
    prev = lambda s: jnp.maximum(s - 1, 0)
    row = lambda s: rows.mod_row(tm)(cur(s))
    z_shape = pltpu.VMEM((n_pieces, tm, piece_cols), F32)
    return pl.pallas_call(
        functools.partial(_prep_kernel, len(xs), rows.n_ctx // tm),
        grid=(n_tiles + 1,),
        in_specs=_source_specs(rows, xs, tm, cur) + [
            _layer_vec_spec(d, layer),
            _mod_block(d, layer, 0, row),
            _mod_block(d, layer, 1, row),
            _const_spec((None, n_pieces, d, piece_cols), lambda s: (j, 0, 0, 0)),
            pl.BlockSpec((None, None, 1, d), lambda s: (0, j, 0, 0)),
            pl.BlockSpec((None, None, 1, d), lambda s: (1, j, 0, 0)),
        ],
        out_specs=[pl.BlockSpec((tm, PK_BLOCKS * d), lambda s: (prev(s), 0)),
                   pl.BlockSpec((tm // CHUNK, DEC_ROWS, d), lambda s: (prev(s), 0, 0))],
        out_shape=[jax.ShapeDtypeStruct((rows.n_rows, PK_BLOCKS * d), BF16),
                   jax.ShapeDtypeStruct((rows.n_rows // CHUNK, DEC_ROWS, d), F32)],
        scratch_shapes=[z_shape, z_shape, pltpu.VMEM((tm, d), BF16)],
        compiler_params=_cparams(("arbitrary",)),
        name="hgrn_prep",
    )(*xs, norm1, mods, mods, w_pieces, lb, lb)


def _gla_tile(qs_ref, ks_ref, v_ref, dec_ref, s_ref, dirn, emit):
    t_rows, d = qs_ref.shape
    reverse = dirn == 1
    n_chunks = t_rows // CHUNK
    r = lax.broadcasted_iota(jnp.int32, (CHUNK, CHUNK), 0)
    c = lax.broadcasted_iota(jnp.int32, (CHUNK, CHUNK), 1)
    mask = (c >= r) if reverse else (c <= r)
    nt = (((1,), (1,)), ((), ()))
    tn = (((0,), (0,)), ((), ()))
    order = range(n_chunks - 1, -1, -1) if reverse else range(n_chunks)
    heads = range(d // HEAD_DIM)
    hsl = lambda h: slice(h * HEAD_DIM, (h + 1) * HEAD_DIM)
    rsl = lambda ci: slice(ci * CHUNK, (ci + 1) * CHUNK)
    p, u = {}, {}
    for h in heads:
        for ci in order:
            sc = lax.dot_general(qs_ref[rsl(ci), hsl(h)], ks_ref[rsl(ci), hsl(h)], nt,
                                 preferred_element_type=F32)
            p[h, ci] = jnp.where(mask, sc, 0.0).astype(BF16)
    for h in heads:
        for ci in order:
            u[h, ci] = lax.dot_general(v_ref[rsl(ci), hsl(h)], ks_ref[rsl(ci), hsl(h)], tn,
                                       preferred_element_type=F32)
    for h in heads:
        st = s_ref[h]
        for ci in order:
            e_ref = dec_ref[ci, 2 * dirn:2 * dirn + 1, hsl(h)]
            e_lr = dec_ref[ci, 2 * dirn + 1:2 * dirn + 2, hsl(h)]
            o = lax.dot_general(qs_ref[rsl(ci), hsl(h)], (st * e_ref).astype(BF16), nt,
                                preferred_element_type=F32)
            o = o + jnp.dot(p[h, ci], v_ref[rsl(ci), hsl(h)], preferred_element_type=F32)
            st = st * (e_ref * e_lr) + u[h, ci] * e_lr
            emit(rsl(ci), hsl(h), o)
        s_ref[h] = st


def _scan_bwd_kernel(qs_ref, ks_ref, v_ref, dec_ref, o_ref, s_ref):
    @pl.when(pl.program_id(1) == 0)
    def _():
        s_ref[...] = jnp.zeros_like(s_ref)

    def emit(rows, hs, o):
        o_ref[rows, hs] = o

    _gla_tile(qs_ref, ks_ref, v_ref, dec_ref, s_ref, 1, emit)


def _scan_fwd_kernel(n_src, ctx_steps, *refs):
    x_refs, refs = _split_sources(n_src, refs)
    (qs_ref, ks_ref, v_ref, dec_ref, gate_ref, ob_ref, gn_ref, g1_ref, wout_ref,
     o_ref, s_ref, y_ref) = refs

    @pl.when(pl.program_id(1) == 0)
    def _():
        s_ref[...] = jnp.zeros_like(s_ref)

    def emit(rows, hs, o):
        o = o + ob_ref[rows, hs]
        o = o * lax.rsqrt(jnp.mean(o * o, axis=-1, keepdims=True) + EPS)
        y_ref[rows, hs] = (o * gn_ref[:, hs] * gate_ref[rows, hs].astype(F32)).astype(BF16)

    _gla_tile(qs_ref, ks_ref, v_ref, dec_ref, s_ref, 0, emit)
    y = jnp.dot(y_ref[...], wout_ref[...], preferred_element_type=F32)
    o_ref[...] = _load_rows(x_refs, pl.program_id(1) < ctx_steps) + g1_ref[...] * y


def _scan_tiles(rows, reverse):
    t = T_SCAN
    nct = rows.ctx_len // t
    ntl = rows.seq // t
    ctx_tiles = rows.n_ctx // t

    def tile(b, s):
        if reverse:
            return jnp.where(s < nct, b * nct + (nct - 1 - s), ctx_tiles + b * ntl + (ntl - 1 - (s - nct)))
        return jnp.where(s < nct, b * nct + s, ctx_tiles + b * ntl + (s - nct))

    mod_row = lambda b, s: jnp.where(s < nct, CTX_MOD_ROW, b)
    return tile, mod_row, nct + ntl


def _hgrn_scan_bwd(rows, packed, dec):
    d = rows.d
    t = T_SCAN
    tile, _, steps = _scan_tiles(rows, True)
    act = pl.BlockSpec((t, d), lambda b, s: (tile(b, s), 0))
    pk = lambda k: pl.BlockSpec((t, d), lambda b, s: (tile(b, s), k))
    return pl.pallas_call(
        _scan_bwd_kernel,
        grid=(rows.bsz, steps),
        in_specs=[pk(PK_QS + 2), pk(PK_KS + 2), pk(PK_V),
                  pl.BlockSpec((t // CHUNK, DEC_ROWS, d), lambda b, s: (tile(b, s), 0, 0))],
        out_specs=act,
        out_shape=jax.ShapeDtypeStruct((rows.n_rows, d), F32),
        scratch_shapes=[pltpu.VMEM((d // HEAD_DIM, HEAD_DIM, HEAD_DIM), F32)],
        compiler_params=_cparams(("arbitrary", "arbitrary")),
        name="hgrn_scan_bwd",
    )(packed, packed, packed, dec)


def _hgrn_scan_fwd(rows, packed, dec, o_bwd, xs, gnorm, mods, w_out, layer, j):
    d = rows.d
    t = T_SCAN
    tile, mod_row, steps = _scan_tiles(rows, False)
    act = pl.BlockSpec((t, d), lambda b, s: (tile(b, s), 0))
    pk = lambda k: pl.BlockSpec((t, d), lambda b, s: (tile(b, s), k))
    return pl.pallas_call(
        functools.partial(_scan_fwd_kernel, len(xs), rows.ctx_len // t),
        grid=(rows.bsz, steps),
        in_specs=_source_specs(rows, xs, t, tile) + [
                  pk(PK_QS), pk(PK_KS), pk(PK_V),
                  pl.BlockSpec((t // CHUNK, DEC_ROWS, d), lambda b, s: (tile(b, s), 0, 0)),
                  pk(PK_GATE), act,
                  _layer_vec_spec(d, j),
                  _mod_block(d, layer, 2, mod_row),
                  _const_spec((None, d, d), lambda b, s: (j, 0, 0))],
        out_specs=act,
        out_shape=jax.ShapeDtypeStruct((rows.n_rows, d), F32),
        scratch_shapes=[pltpu.VMEM((d // HEAD_DIM, HEAD_DIM, HEAD_DIM), F32),
                        pltpu.VMEM((t, d), BF16)],
        compiler_params=_cparams(("arbitrary", "arbitrary")),
        name="hgrn_scan_fwd",
    )(*xs, packed, packed, packed, dec, packed, o_bwd, gnorm, mods, w_out)


def _conv_taps(cw_ref, cb_ref, um, u, up):
    return cb_ref[...] + cw_ref[0:1, :] * um + cw_ref[1:2, :] * u + cw_ref[2:3, :] * up


def _conv_seq_kernel(n_ctx_tiles, ctx_len, x_ref, gain_ref, sh_ref, sc_ref, g1_ref, win_ref,
                     cw_ref, cb_ref, wout_ref, o_ref):
    x = x_ref[...]
    tm, d = x.shape
    h = _rms_mod(x, gain_ref[...], sh_ref[...], sc_ref[...]).astype(BF16)
    p = jnp.dot(h, win_ref[...], preferred_element_type=F32)
    gate_b, u = p[:, :d], p[:, d:2 * d] * p[:, 2 * d:]
    period = jnp.where(pl.program_id(0) < n_ctx_tiles, ctx_len, GRID_W)
    pos = lax.broadcasted_iota(jnp.int32, (tm, 1), 0) & (period - 1)
    um = jnp.where(pos == 0, 0.0, pltpu.roll(u, 1, axis=0))
    up = jnp.where(pos == period - 1, 0.0, pltpu.roll(u, tm - 1, axis=0))
    cv = _conv_taps(cw_ref, cb_ref, um, u, up)
    y = jnp.dot((gate_b * cv).astype(BF16), wout_ref[...], preferred_element_type=F32)
    o_ref[...] = x + g1_ref[...] * y


def _conv_rows_kernel(n_ctx_tiles, tiles_per_batch, xp_ref, x_ref, xn_ref, gain_ref, sh_ref,
                      sc_ref, g1_ref, win_ref, cw_ref, cb_ref, wout_ref, o_ref, h_ref):
    x = x_ref[...]
    tm, d = x.shape
    gain, sh, sc = gain_ref[...], sh_ref[...], sc_ref[...]
    h_ref[0:GRID_W, :] = _rms_mod(xp_ref[...], gain, sh, sc).astype(BF16)
    h_ref[GRID_W:GRID_W + tm, :] = _rms_mod(x, gain, sh, sc).astype(BF16)
    h_ref[GRID_W + tm:, :] = _rms_mod(xn_ref[...], gain, sh, sc).astype(BF16)
    p = jnp.dot(h_ref[...], win_ref[:, d:], preferred_element_type=F32)
    u = p[:, :d] * p[:, d:]
    gate_b = jnp.dot(h_ref[GRID_W:GRID_W + tm, :], win_ref[:, :d], preferred_element_type=F32)
    i = pl.program_id(0)
    jl = (i - n_ctx_tiles) % tiles_per_batch
    is_lat = i >= n_ctx_tiles
    has_prev = jnp.logical_and(is_lat, jl != 0)
    has_next = jnp.logical_and(is_lat, jl != tiles_per_batch - 1)
    row = lax.broadcasted_iota(jnp.int32, (tm, 1), 0)
    um = jnp.where(jnp.logical_or(row >= GRID_W, has_prev), u[0:tm], 0.0)
    up = jnp.where(jnp.logical_or(row < tm - GRID_W, has_next), u[2 * GRID_W:], 0.0)
    cv = _conv_taps(cw_ref, cb_ref, um, u[GRID_W:GRID_W + tm], up)
    y = jnp.dot((gate_b * cv).astype(BF16), wout_ref[...], preferred_element_type=F32)
    o_ref[...] = x + g1_ref[...] * y


def _conv_mixer(rows, x2, norm1, mods, w_in, cw, cb, w_out, layer, j, along_rows):
    d = rows.d
    tm = TM_CONV
    row = rows.mod_row(tm)
    n_tiles = rows.n_rows // tm
    n_ctx_tiles = rows.n_ctx // tm
    common = [
        _layer_vec_spec(d, layer),
        _mod_block(d, layer, 0, row),
        _mod_block(d, layer, 1, row),
        _mod_block(d, layer, 2, row),
        _const_spec((None, d, 3 * d), lambda i: (j, 0, 0)),
        pl.BlockSpec((None, 3, d), lambda i: (j, 0, 0)),
        _layer_vec_spec(d, j),
        _const_spec((None, d, d), lambda i: (j, 0, 0)),
    ]
    args = (norm1, mods, mods, mods, w_in, cw, cb, w_out)
    if along_rows:
        sub = tm // GRID_W
        last = rows.n_rows // GRID_W - 1
        kern = functools.partial(_conv_rows_kernel, n_ctx_tiles, rows.seq // tm)
        in_specs = [pl.BlockSpec((GRID_W, d), lambda i: (jnp.maximum(i * sub - 1, 0), 0)),
                    pl.BlockSpec((tm, d), lambda i: (i, 0)),
                    pl.BlockSpec((GRID_W, d), lambda i: (jnp.minimum((i + 1) * sub, last), 0))]
        args = (x2, x2, x2) + args
        scratch = [pltpu.VMEM((tm + 2 * GRID_W, d), BF16)]
        name = "conv_mixer_rows"
    else:
        kern = functools.partial(_conv_seq_kernel, n_ctx_tiles, rows.ctx_len)
        in_specs = [pl.BlockSpec((tm, d), lambda i: (i, 0))]
        args = (x2,) + args
        scratch = []
        name = "conv_mixer_seq"
    return pl.pallas_call(
        kern,
        grid=(n_tiles,),
        in_specs=in_specs + common,
        out_specs=pl.BlockSpec((tm, d), lambda i: (i, 0)),
        out_shape=jax.ShapeDtypeStruct((rows.n_rows, d), F32),
        scratch_shapes=scratch,
        compiler_params=_cparams(("parallel",)),
        name=name,
    )(*args)


def _mlp_residual(x_ref, gain_ref, sh_ref, sc_ref, g2_ref, w1_ref, w2_ref):
    x = x_ref[...]
    h = _rms_mod(x, gain_ref[...], sh_ref[...], sc_ref[...]).astype(BF16)
    d_ff = w1_ref.shape[-1]
    y = None
    for lo in range(0, d_ff, FF_CHUNK):
        a = jnp.dot(h, w1_ref[:, lo:lo + FF_CHUNK], preferred_element_type=F32)
        a = jnp.square(jnp.maximum(a, 0.0)).astype(BF16)
        part = jnp.dot(a, w2_ref[lo:lo + FF_CHUNK, :], preferred_element_type=F32)
        y = part if y is None else y + part
    return x + g2_ref[...] * y


def _mlp_kernel(x_ref, gain_ref, sh_ref, sc_ref, g2_ref, w1_ref, w2_ref, o_ref):
    o_ref[...] = _mlp_residual(x_ref, gain_ref, sh_ref, sc_ref, g2_ref, w1_ref, w2_ref)


def _mlp_final_kernel(x_ref, gain_ref, sh_ref, sc_ref, g2_ref, w1_ref, w2_ref, gf_ref, o_ref):
    x = _mlp_residual(x_ref, gain_ref, sh_ref, sc_ref, g2_ref, w1_ref, w2_ref)
    o_ref[...] = x * lax.rsqrt(jnp.mean(x * x, axis=-1, keepdims=True) + EPS) * gf_ref[...]


def _mlp(rows, x2, norm2, mods, w1, w2, layer, norm_f=None):
    d = rows.d
    d_ff = w1.shape[-1]
    tm = TM_MLP
    final = norm_f is not None
    first = rows.n_ctx // tm if final else 0
    n_out = rows.n_lat if final else rows.n_rows
    row = lambda i: rows.mod_row(tm)(i + first)
    in_specs = [
        pl.BlockSpec((tm, d), lambda i: (i + first, 0)),
        _layer_vec_spec(d, layer),
        _mod_block(d, layer, 3, row),
        _mod_block(d, layer, 4, row),
        _mod_block(d, layer, 5, row),
        _const_spec((None, d, d_ff), lambda i: (layer, 0, 0)),
        _const_spec((None, d_ff, d), lambda i: (layer, 0, 0)),
    ]
    args = (x2, norm2, mods, mods, mods, w1, w2)
    if final:
        in_specs.append(pl.BlockSpec((1, d), lambda i: (0, 0)))
        args += (norm_f.reshape(1, d),)
    return pl.pallas_call(
        _mlp_final_kernel if final else _mlp_kernel,
        grid=(n_out // tm,),
        in_specs=in_specs,
        out_specs=pl.BlockSpec((tm, d), lambda i: (i, 0)),
        out_shape=jax.ShapeDtypeStruct((n_out, d), F32),
        compiler_params=_cparams(("parallel",)),
        name="mlp_final" if final else "mlp",
    )(*args)


def kernel(x, c, ctx, c_ctx, ada_w, ada_b, norm1, norm2, norm_f, mlp_w1, mlp_w2, hgrn_w_in, hgrn_lb,
           hgrn_gnorm, hgrn_w_out, conv_w_in, conv_w, conv_b, conv_w_out):
    bsz, seq, d = x.shape
    ctx_len = ctx.shape[1]
    depth = ada_w.shape[0]
    assert bsz <= CTX_MOD_ROW and d % HEAD_DIM == 0
    assert seq % max(TM_MLP, TM_CONV, T_SCAN) == 0 and ctx_len % T_SCAN == 0
    assert (bsz * ctx_len) % max(TM_MLP, TM_CONV) == 0
    assert ctx_len & (ctx_len - 1) == 0 and TM_CONV % ctx_len == 0
    rows = _Rows(bsz, seq, ctx_len, d)

    c8 = jnp.concatenate([c, c_ctx[None, :], jnp.zeros((MOD_ROWS - bsz - 1, d), F32)], axis=0)
    mods = _ada_table(c8, ada_w, ada_b).reshape(depth * MOD_ROWS * N_MODS, 1, d)
    lb = _lower_bounds(hgrn_lb).reshape(2, -1, 1, d)

    norm1 = norm1.reshape(depth, 1, d)
    norm2 = norm2.reshape(depth, 1, d)
    gnorm = hgrn_gnorm.reshape(-1, 1, d)
    conv_b = conv_b.reshape(-1, 1, d)
    w1, w2 = mlp_w1.astype(BF16), mlp_w2.astype(BF16)
    hw_in, hw_out = _prep_weight_pieces(hgrn_w_in), hgrn_w_out.astype(BF16)
    cw_in, cw_out = conv_w_in.astype(BF16), conv_w_out.astype(BF16)

    xs = (ctx.reshape(bsz * ctx_len, d), x.reshape(bsz * seq, d))
    for i in range(depth):
        j = i // N_MIXERS
        if i % N_MIXERS == 0:
            packed, dec = _hgrn_prep(rows, xs, norm1, mods, hw_in, lb, i, j)
            o_bwd = _hgrn_scan_bwd(rows, packed, dec)
            x2 = _hgrn_scan_fwd(rows, packed, dec, o_bwd, xs, gnorm, mods, hw_out, i, j)
        else:
            x2 = _conv_mixer(rows, x2, norm1, mods, cw_in, conv_w, conv_b, cw_out, i, j,
                             along_rows=(j % 2 == 1))
        x2 = _mlp(rows, x2, norm2, mods, w1, w2, i, norm_f if i == depth - 1 else None)
        xs = (x2,)
    return x2.reshape(bsz, seq, d)
```

```python
import functools

import jax
import jax.numpy as jnp
from jax import lax
from jax.experimental import pallas as pl
from jax.experimental.pallas import tpu as pltpu

F32 = jnp.float32
BF16 = jnp.bfloat16

GRID_W = 64
HEAD_DIM = 128
CHUNK = 64
EPS = 1e-6
N_MIXERS = 2
CTX_MOD_ROW = 4
MOD_ROWS = 8
N_MODS = 6
DEC_ROWS = 8

VMEM_LIMIT = 56 * 1024 * 1024

TM_PREP = 512
PREP_PIECES = 4
PK_QS, PK_KS, PK_V, PK_GATE, PK_BLOCKS = 0, 1, 4, 5, 6
TM_MLP = 1024
FF_CHUNK = 1024
TM_CONV = 1024
T_SCAN = 512
TN_ADA = 2048


def _cparams(sem):
    return pltpu.CompilerParams(dimension_semantics=sem, vmem_limit_bytes=VMEM_LIMIT)


def _const_spec(shape, index_map):
    return pl.BlockSpec(shape, index_map, pipeline_mode=pl.Buffered(1))


def _silu(t):
    return t * jax.nn.sigmoid(t)


def _rms_mod(x, gain, shift, scale):
    y = x * lax.rsqrt(jnp.mean(x * x, axis=-1, keepdims=True) + EPS) * gain
    return y * (1.0 + scale) + shift


def _ada_kernel(c_ref, w_ref, b_ref, o_ref):
    s = _silu(c_ref[...]).astype(BF16)
    o_ref[...] = jnp.dot(s, w_ref[...].astype(BF16), preferred_element_type=F32) + b_ref[...]


def _ada_table(c8, ada_w, ada_b):
    depth, d, n = ada_w.shape
    return pl.pallas_call(
        _ada_kernel,
        grid=(depth, n // TN_ADA),
        in_specs=[
            pl.BlockSpec((MOD_ROWS, d), lambda l, j: (0, 0)),
            pl.BlockSpec((None, d, TN_ADA), lambda l, j: (l, 0, j)),
            pl.BlockSpec((None, 1, TN_ADA), lambda l, j: (l, 0, j)),
        ],
        out_specs=pl.BlockSpec((None, MOD_ROWS, TN_ADA), lambda l, j: (l, 0, j)),
        out_shape=jax.ShapeDtypeStruct((depth, MOD_ROWS, n), F32),
        compiler_params=_cparams(("parallel", "parallel")),
        name="ada_table",
    )(c8, ada_w, ada_b.reshape(depth, 1, n))


def _lb_kernel(x_ref, o_ref):
    n_rec = x_ref.shape[1]
    for d in range(x_ref.shape[0]):
        rows = [x_ref[d, j:j + 1, :] for j in range(n_rec)]
        m = functools.reduce(jnp.maximum, rows)
        e = [jnp.exp(r - m) for r in rows]
        tot = functools.reduce(lambda a, b: a + b, e)
        acc = None
        for j in range(n_rec):
            p = e[j] / tot
            acc = p if acc is None else acc + p
            o_ref[d, j:j + 1, :] = acc - e[0] / tot


def _lower_bounds(hgrn_lb):
    return pl.pallas_call(
        _lb_kernel,
        out_shape=jax.ShapeDtypeStruct(hgrn_lb.shape, F32),
        name="hgrn_lower_bounds",
    )(hgrn_lb)


class _Rows:
    def __init__(self, bsz, seq, ctx_len, d):
        self.bsz, self.seq, self.ctx_len, self.d = bsz, seq, ctx_len, d
        self.n_ctx = bsz * ctx_len
        self.n_lat = bsz * seq
        self.n_rows = self.n_ctx + self.n_lat

    def mod_row(self, tm):
        n_ctx_tiles = self.n_ctx // tm
        per_batch = self.seq // tm
        return lambda i: jnp.where(i < n_ctx_tiles, CTX_MOD_ROW, (i - n_ctx_tiles) // per_batch)


def _mod_block(d, layer, k, row_of_grid):
    return pl.BlockSpec((None, 1, d),
                        lambda *g: ((layer * MOD_ROWS + row_of_grid(*g)) * N_MODS + k, 0, 0))


def _layer_vec_spec(d, layer):
    return pl.BlockSpec((None, 1, d), lambda *g: (layer, 0, 0))


def _cumsum_rows(g, reverse):
    n = g.shape[0]
    row = lax.broadcasted_iota(jnp.int32, g.shape, 0)
    s = 1
    while s < n:
        if reverse:
            g = g + jnp.where(row < n - s, pltpu.roll(g, n - s, axis=0), 0.0)
        else:
            g = g + jnp.where(row >= s, pltpu.roll(g, s, axis=0), 0.0)
        s *= 2
    return g


def _split_sources(n_src, refs):
    return refs[:n_src], refs[n_src:]


def _load_rows(x_refs, is_ctx):
    if len(x_refs) == 1:
        return x_refs[0][...]
    return jnp.where(is_ctx, x_refs[0][...], x_refs[1][...])


def _prep_kernel(n_src, ctx_tiles, *refs):
    x_refs, refs = _split_sources(n_src, refs)
    (gain_ref, sh_ref, sc_ref, w_ref, lbf_ref, lbb_ref,
     pk_ref, dec_ref, za_ref, zb_ref, h_ref) = refs
    tm, d = h_ref.shape
    n_chunks = tm // CHUNK
    n_units = za_ref.shape[0]
    wpiece = za_ref.shape[-1]
    n_parts = n_units * wpiece // d
    ws = d // n_units
    step = pl.program_id(0)
    lb_refs = (lbf_ref, lbb_ref)
    assert n_parts == 5 and wpiece == n_parts * ws

    @pl.when(step == 0)
    def _():
        zb_ref[...] = jnp.zeros(zb_ref.shape, F32)

    def body(zw_ref, zr_ref):
        x = _load_rows(x_refs, step < ctx_tiles)
        h_ref[...] = _rms_mod(x, gain_ref[...], sh_ref[...], sc_ref[...]).astype(BF16)
        dec_ref[:, 4:, :] = jnp.zeros((n_chunks, DEC_ROWS - 4, d), F32)
        for ci in range(n_chunks):
            if ci % (n_chunks // n_units) == 0:
                it = ci // (n_chunks // n_units)
                zw_ref[it] = jnp.dot(h_ref[...], w_ref[it], preferred_element_type=F32)
            rows = slice(ci * CHUNK, (ci + 1) * CHUNK)
            for j in range(n_units):
                cols = slice(j * ws, (j + 1) * ws)

                def part(p):
                    piece, k = divmod(p * n_units + j, n_parts)
                    return zr_ref[piece, rows, k * ws:(k + 1) * ws]

                def put(block, val):
                    pk_ref[rows, block * d + j * ws:block * d + (j + 1) * ws] = val.astype(BF16)

                put(PK_V, part(2))
                put(PK_GATE, _silu(part(4)))
                q = _silu(part(3))
                for dirn, lb_ref in enumerate(lb_refs):
                    lb = lb_ref[:, cols]
                    f = lb + (1.0 - lb) * jax.nn.sigmoid(part(dirn))
                    cum = _cumsum_rows(jnp.log(f), dirn == 1)
                    if dirn == 1:
                        ref, last = cum[CHUNK // 2:CHUNK // 2 + 1], cum[0:1]
                    else:
                        ref, last = cum[CHUNK // 2 - 1:CHUNK // 2], cum[CHUNK - 1:CHUNK]
                    put(PK_QS + 2 * dirn, q * jnp.exp(cum - ref))
                    put(PK_KS + 2 * dirn, (1.0 - f) * jnp.exp(ref - cum))
                    dec_ref[ci, 2 * dirn:2 * dirn + 1, cols] = jnp.exp(ref)
                    dec_ref[ci, 2 * dirn + 1:2 * dirn + 2, cols] = jnp.exp(last - ref)

    @pl.when(step % 2 == 0)
    def _():
        body(za_ref, zb_ref)

    @pl.when(step % 2 == 1)
    def _():
        body(zb_ref, za_ref)


def _source_specs(rows, xs, tm, tile_of_grid):
    d = rows.d
    if len(xs) == 1:
        return [pl.BlockSpec((tm, d), lambda *g: (tile_of_grid(*g), 0))]
    n_ctx_tiles = rows.n_ctx // tm
    return [pl.BlockSpec((tm, d), lambda *g: (jnp.minimum(tile_of_grid(*g), n_ctx_tiles - 1), 0)),
            pl.BlockSpec((tm, d), lambda *g: (jnp.maximum(tile_of_grid(*g) - n_ctx_tiles, 0), 0))]


def _cast_kernel(w_ref, o_ref):
    o_ref[...] = w_ref[...].astype(o_ref.dtype)


def _prep_weight_pieces(w_in):
    n_layers, d, n = w_in.shape
    wp = n // PREP_PIECES
    return pl.pallas_call(
        _cast_kernel,
        grid=(n_layers, PREP_PIECES),
        in_specs=[pl.BlockSpec((None, d, wp), lambda l, i: (l, 0, i))],
        out_specs=pl.BlockSpec((None, None, d, wp), lambda l, i: (l, i, 0, 0)),
        out_shape=jax.ShapeDtypeStruct((n_layers, PREP_PIECES, d, wp), BF16),
        compiler_params=_cparams(("parallel", "parallel")),
        name="prep_weight_pieces",
    )(w_in)


def _hgrn_prep(rows, xs, norm1, mods, w_pieces, lb, layer, j):
    d = rows.d
    tm = TM_PREP
    n_pieces, _, piece_cols = w_pieces.shape[1:]
    n_tiles = rows.n_rows // tm
    cur = lambda s: jnp.minimum(s, n_tiles - 1)
    prev = lambda s: jnp.maximum(s - 1, 0)
    row = lambda s: rows.mod_row(tm)(cur(s))
    z_shape = pltpu.VMEM((n_pieces, tm, piece_cols), F32)
    return pl.pallas_call(
        functools.partial(_prep_kernel, len(xs), rows.n_ctx // tm),
        grid=(n_tiles + 1,),
        in_specs=_source_specs(rows, xs, tm, cur) + [
            _layer_vec_spec(d, layer),
            _mod_block(d, layer, 0, row),
            _mod_block(d, layer, 1, row),
            _const_spec((None, n_pieces, d, piece_cols), lambda s: (j, 0, 0, 0)),
            pl.BlockSpec((None, None, 1, d), lambda s: (0, j, 0, 0)),
            pl.BlockSpec((None, None, 1, d), lambda s: (1, j, 0, 0)),
        ],
        out_specs=[pl.BlockSpec((tm, PK_BLOCKS * d), lambda s: (prev(s), 0)),
                   pl.BlockSpec((tm // CHUNK, DEC_ROWS, d), lambda s: (prev(s), 0, 0))],
        out_shape=[jax.ShapeDtypeStruct((rows.n_rows, PK_BLOCKS * d), BF16),
                   jax.ShapeDtypeStruct((rows.n_rows // CHUNK, DEC_ROWS, d), F32)],
        scratch_shapes=[z_shape, z_shape, pltpu.VMEM((tm, d), BF16)],
        compiler_params=_cparams(("arbitrary",)),
        name="hgrn_prep",
    )(*xs, norm1, mods, mods, w_pieces, lb, lb)


def _gla_tile(qs_ref, ks_ref, v_ref, dec_ref, s_ref, dirn, emit):
    t_rows, d = qs_ref.shape
    reverse = dirn == 1
    n_chunks = t_rows // CHUNK
    r = lax.broadcasted_iota(jnp.int32, (CHUNK, CHUNK), 0)
    c = lax.broadcasted_iota(jnp.int32, (CHUNK, CHUNK), 1)
    mask = (c >= r) if reverse else (c <= r)
    nt = (((1,), (1,)), ((), ()))
    tn = (((0,), (0,)), ((), ()))
    order = range(n_chunks - 1, -1, -1) if reverse else range(n_chunks)
    heads = range(d // HEAD_DIM)
    hsl = lambda h: slice(h * HEAD_DIM, (h + 1) * HEAD_DIM)
    rsl = lambda ci: slice(ci * CHUNK, (ci + 1) * CHUNK)
    p, u = {}, {}
    for h in heads:
        for ci in order:
            sc = lax.dot_general(qs_ref[rsl(ci), hsl(h)], ks_ref[rsl(ci), hsl(h)], nt,
                                 preferred_element_type=F32)
            p[h, ci] = jnp.where(mask, sc, 0.0).astype(BF16)
    for h in heads:
        for ci in order:
            u[h, ci] = lax.dot_general(v_ref[rsl(ci), hsl(h)], ks_ref[rsl(ci), hsl(h)], tn,
                                       preferred_element_type=F32)
    for h in heads:
        st = s_ref[h]
        for ci in order:
            e_ref = dec_ref[ci, 2 * dirn:2 * dirn + 1, hsl(h)]
            e_lr = dec_ref[ci, 2 * dirn + 1:2 * dirn + 2, hsl(h)]
            o = lax.dot_general(qs_ref[rsl(ci), hsl(h)], (st * e_ref).astype(BF16), nt,
                                preferred_element_type=F32)
            o = o + jnp.dot(p[h, ci], v_ref[rsl(ci), hsl(h)], preferred_element_type=F32)
            st = st * (e_ref * e_lr) + u[h, ci] * e_lr
            emit(rsl(ci), hsl(h), o)
        s_ref[h] = st


def _scan_bwd_kernel(qc, kc, vc, dc, ql, kl, vl, dl, oc_ref, ol_ref, s_ref):
    def run(qs_ref, ks_ref, v_ref, dec_ref, o_ref):
        def emit(rows, hs, o):
            o_ref[rows, hs] = o

        _gla_tile(qs_ref, ks_ref, v_ref, dec_ref, s_ref, 1, emit)

    @pl.when(pl.program_id(1) == 0)
    def _():
        s_ref[...] = jnp.zeros_like(s_ref)
        run(qc, kc, vc, dc, oc_ref)

    @pl.when(pl.program_id(1) > 0)
    def _():
        run(ql, kl, vl, dl, ol_ref)


def _scan_fwd_kernel(qc, kc, vc, dc, gc, obc, xc, ql, kl, vl, dl, gl, obl, xl,
                     gn_ref, g1_ref, wout_ref, oc_ref, ol_ref, s_ref, y_ref):
    def run(qs_ref, ks_ref, v_ref, dec_ref, gate_ref, ob_ref, x_ref, o_ref):
        n = qs_ref.shape[0]

        def emit(rows, hs, o):
            o = o + ob_ref[rows, hs]
            o = o * lax.rsqrt(jnp.mean(o * o, axis=-1, keepdims=True) + EPS)
            y_ref[rows, hs] = (o * gn_ref[:, hs] * gate_ref[rows, hs].astype(F32)).astype(BF16)

        _gla_tile(qs_ref, ks_ref, v_ref, dec_ref, s_ref, 0, emit)
        y = jnp.dot(y_ref[0:n, :], wout_ref[...], preferred_element_type=F32)
        o_ref[...] = x_ref[...] + g1_ref[...] * y

    @pl.when(pl.program_id(1) == 0)
    def _():
        s_ref[...] = jnp.zeros_like(s_ref)
        run(qc, kc, vc, dc, gc, obc, xc, oc_ref)

    @pl.when(pl.program_id(1) > 0)
    def _():
        run(ql, kl, vl, dl, gl, obl, xl, ol_ref)


class _ScanBlocks:
    def __init__(self, rows, reverse):
        self.rows, self.t = rows, T_SCAN
        self.n_lat = rows.seq // self.t
        self.steps = 1 + self.n_lat
        n_lat = self.n_lat
        if reverse:
            self.lat = lambda b, s: b * n_lat + (n_lat - 1 - jnp.maximum(s - 1, 0))
        else:
            self.lat = lambda b, s: b * n_lat + jnp.maximum(s - 1, 0)
        self.flat_lat0 = rows.n_ctx // self.t

    def flat(self, width, col=0):
        r, t, off, lat = self.rows, self.t, self.flat_lat0, self.lat
        return [pl.BlockSpec((r.ctx_len, width), lambda b, s: (b, col)),
                pl.BlockSpec((t, width), lambda b, s: (off + lat(b, s), col))]

    def split(self, width):
        r, t, lat = self.rows, self.t, self.lat
        return [pl.BlockSpec((r.ctx_len, width), lambda b, s: (b, 0)),
                pl.BlockSpec((t, width), lambda b, s: (lat(b, s), 0))]

    def dec(self):
        r, t, off, lat = self.rows, self.t, self.flat_lat0, self.lat
        d = r.d
        return [pl.BlockSpec((r.ctx_len // CHUNK, DEC_ROWS, d), lambda b, s: (b, 0, 0)),
                pl.BlockSpec((t // CHUNK, DEC_ROWS, d), lambda b, s: (off + lat(b, s), 0, 0))]

    def out_shapes(self, dtype):
        r = self.rows
        return [jax.ShapeDtypeStruct((r.n_ctx, r.d), dtype), jax.ShapeDtypeStruct((r.n_lat, r.d), dtype)]


def _interleave(*pairs):
    return [p[0] for p in pairs] + [p[1] for p in pairs]


def _hgrn_scan_bwd(rows, packed, dec):
    d = rows.d
    blk = _ScanBlocks(rows, True)
    pk = lambda k: blk.flat(d, k)
    return pl.pallas_call(
        _scan_bwd_kernel,
        grid=(rows.bsz, blk.steps),
        in_specs=_interleave(pk(PK_QS + 2), pk(PK_KS + 2), pk(PK_V), blk.dec()),
        out_specs=blk.split(d),
        out_shape=blk.out_shapes(F32),
        scratch_shapes=[pltpu.VMEM((d // HEAD_DIM, HEAD_DIM, HEAD_DIM), F32)],
        compiler_params=_cparams(("arbitrary", "arbitrary")),
        name="hgrn_scan_bwd",
    )(*([packed, packed, packed, dec] * 2))


def _hgrn_scan_fwd(rows, packed, dec, o_bwd, xs, gnorm, mods, w_out, layer, j):
    d = rows.d
    blk = _ScanBlocks(rows, False)
    pk = lambda k: blk.flat(d, k)
    x_pair = blk.split(d) if len(xs) == 2 else blk.flat(d)
    x_args = list(xs) if len(xs) == 2 else [xs[0], xs[0]]
    mod_row = lambda b, s: jnp.where(s == 0, CTX_MOD_ROW, b)
    pairs = _interleave(pk(PK_QS), pk(PK_KS), pk(PK_V), blk.dec(), pk(PK_GATE), blk.split(d), x_pair)
    half = [packed, packed, packed, dec, packed]
    return pl.pallas_call(
        _scan_fwd_kernel,
        grid=(rows.bsz, blk.steps),
        in_specs=pairs + [_layer_vec_spec(d, j),
                          _mod_block(d, layer, 2, mod_row),
                          _const_spec((None, d, d), lambda b, s: (j, 0, 0))],
        out_specs=blk.split(d),
        out_shape=blk.out_shapes(F32),
        scratch_shapes=[pltpu.VMEM((d // HEAD_DIM, HEAD_DIM, HEAD_DIM), F32),
                        pltpu.VMEM((max(blk.t, rows.ctx_len), d), BF16)],
        compiler_params=_cparams(("arbitrary", "arbitrary")),
        name="hgrn_scan_fwd",
    )(*half, o_bwd[0], x_args[0], *half, o_bwd[1], x_args[1], gnorm, mods, w_out)


def _conv_taps(cw_ref, cb_ref, um, u, up):
    return cb_ref[...] + cw_ref[0:1, :] * um + cw_ref[1:2, :] * u + cw_ref[2:3, :] * up


def _conv_seq_kernel(n_ctx_tiles, ctx_len, x_ref, gain_ref, sh_ref, sc_ref, g1_ref, win_ref,
                     cw_ref, cb_ref, wout_ref, o_ref):
    x = x_ref[...]
    tm, d = x.shape
    h = _rms_mod(x, gain_ref[...], sh_ref[...], sc_ref[...]).astype(BF16)
    p = jnp.dot(h, win_ref[...], preferred_element_type=F32)
    gate_b, u = p[:, :d], p[:, d:2 * d] * p[:, 2 * d:]
    period = jnp.where(pl.program_id(0) < n_ctx_tiles, ctx_len, GRID_W)
    pos = lax.broadcasted_iota(jnp.int32, (tm, 1), 0) & (period - 1)
    um = jnp.where(pos == 0, 0.0, pltpu.roll(u, 1, axis=0))
    up = jnp.where(pos == period - 1, 0.0, pltpu.roll(u, tm - 1, axis=0))
    cv = _conv_taps(cw_ref, cb_ref, um, u, up)
    y = jnp.dot((gate_b * cv).astype(BF16), wout_ref[...], preferred_element_type=F32)
    o_ref[...] = x + g1_ref[...] * y


def _conv_rows_kernel(n_ctx_tiles, tiles_per_batch, xp_ref, x_ref, xn_ref, gain_ref, sh_ref,
                      sc_ref, g1_ref, win_ref, cw_ref, cb_ref, wout_ref, o_ref, h_ref):
    x = x_ref[...]
    tm, d = x.shape
    gain, sh, sc = gain_ref[...], sh_ref[...], sc_ref[...]
    h_ref[0:GRID_W, :] = _rms_mod(xp_ref[...], gain, sh, sc).astype(BF16)
    h_ref[GRID_W:GRID_W + tm, :] = _rms_mod(x, gain, sh, sc).astype(BF16)
    h_ref[GRID_W + tm:, :] = _rms_mod(xn_ref[...], gain, sh, sc).astype(BF16)
    p = jnp.dot(h_ref[...], win_ref[:, d:], preferred_element_type=F32)
    u = p[:, :d] * p[:, d:]
    gate_b = jnp.dot(h_ref[GRID_W:GRID_W + tm, :], win_ref[:, :d], preferred_element_type=F32)
    i = pl.program_id(0)
    jl = (i - n_ctx_tiles) % tiles_per_batch
    is_lat = i >= n_ctx_tiles
    has_prev = jnp.logical_and(is_lat, jl != 0)
    has_next = jnp.logical_and(is_lat, jl != tiles_per_batch - 1)
    row = lax.broadcasted_iota(jnp.int32, (tm, 1), 0)
    um = jnp.where(jnp.logical_or(row >= GRID_W, has_prev), u[0:tm], 0.0)
    up = jnp.where(jnp.logical_or(row < tm - GRID_W, has_next), u[2 * GRID_W:], 0.0)
    cv = _conv_taps(cw_ref, cb_ref, um, u[GRID_W:GRID_W + tm], up)
    y = jnp.dot((gate_b * cv).astype(BF16), wout_ref[...], preferred_element_type=F32)
    o_ref[...] = x + g1_ref[...] * y


def _conv_mixer(rows, x2, norm1, mods, w_in, cw, cb, w_out, layer, j, along_rows):
    d = rows.d
    tm = TM_CONV
    row = rows.mod_row(tm)
    n_tiles = rows.n_rows // tm
    n_ctx_tiles = rows.n_ctx // tm
    common = [
        _layer_vec_spec(d, layer),
        _mod_block(d, layer, 0, row),
        _mod_block(d, layer, 1, row),
        _mod_block(d, layer, 2, row),
        _const_spec((None, d, 3 * d), lambda i: (j, 0, 0)),
        pl.BlockSpec((None, 3, d), lambda i: (j, 0, 0)),
        _layer_vec_spec(d, j),
        _const_spec((None, d, d), lambda i: (j, 0, 0)),
    ]
    args = (norm1, mods, mods, mods, w_in, cw, cb, w_out)
    if along_rows:
        sub = tm // GRID_W
        last = rows.n_rows // GRID_W - 1
        kern = functools.partial(_conv_rows_kernel, n_ctx_tiles, rows.seq // tm)
        in_specs = [pl.BlockSpec((GRID_W, d), lambda i: (jnp.maximum(i * sub - 1, 0), 0)),
                    pl.BlockSpec((tm, d), lambda i: (i, 0)),
                    pl.BlockSpec((GRID_W, d), lambda i: (jnp.minimum((i + 1) * sub, last), 0))]
        args = (x2, x2, x2) + args
        scratch = [pltpu.VMEM((tm + 2 * GRID_W, d), BF16)]
        name = "conv_mixer_rows"
    else:
        kern = functools.partial(_conv_seq_kernel, n_ctx_tiles, rows.ctx_len)
        in_specs = [pl.BlockSpec((tm, d), lambda i: (i, 0))]
        args = (x2,) + args
        scratch = []
        name = "conv_mixer_seq"
    return pl.pallas_call(
        kern,
        grid=(n_tiles,),
        in_specs=in_specs + common,
        out_specs=pl.BlockSpec((tm, d), lambda i: (i, 0)),
        out_shape=jax.ShapeDtypeStruct((rows.n_rows, d), F32),
        scratch_shapes=scratch,
        compiler_params=_cparams(("parallel",)),
        name=name,
    )(*args)


def _mlp_residual(x, gain_ref, sh_ref, sc_ref, g2_ref, w1_ref, w2_ref):
    h = _rms_mod(x, gain_ref[...], sh_ref[...], sc_ref[...]).astype(BF16)
    d_ff = w1_ref.shape[-1]
    y = None
    for lo in range(0, d_ff, FF_CHUNK):
        a = jnp.dot(h, w1_ref[:, lo:lo + FF_CHUNK], preferred_element_type=F32)
        a = jnp.square(jnp.maximum(a, 0.0)).astype(BF16)
        part = jnp.dot(a, w2_ref[lo:lo + FF_CHUNK, :], preferred_element_type=F32)
        y = part if y is None else y + part
    return x + g2_ref[...] * y


def _mlp_kernel(n_src, ctx_tiles, *refs):
    x_refs, refs = _split_sources(n_src, refs)
    x = _load_rows(x_refs, pl.program_id(0) < ctx_tiles)
    refs[-1][...] = _mlp_residual(x, *refs[:-1])


def _mlp_final_kernel(x_ref, gain_ref, sh_ref, sc_ref, g2_ref, w1_ref, w2_ref, gf_ref, o_ref):
    x = _mlp_residual(x_ref[...], gain_ref, sh_ref, sc_ref, g2_ref, w1_ref, w2_ref)
    o_ref[...] = x * lax.rsqrt(jnp.mean(x * x, axis=-1, keepdims=True) + EPS) * gf_ref[...]


def _mlp(rows, xs, norm2, mods, w1, w2, layer, norm_f=None):
    d = rows.d
    d_ff = w1.shape[-1]
    tm = TM_MLP
    final = norm_f is not None
    assert not (final and len(xs) != 1)
    first = rows.n_ctx // tm if final else 0
    n_out = rows.n_lat if final else rows.n_rows
    row = lambda i: rows.mod_row(tm)(i + first)
    in_specs = _source_specs(rows, xs, tm, lambda i: i + first) + [
        _layer_vec_spec(d, layer),
        _mod_block(d, layer, 3, row),
        _mod_block(d, layer, 4, row),
        _mod_block(d, layer, 5, row),
        _const_spec((None, d, d_ff), lambda i: (layer, 0, 0)),
        _const_spec((None, d_ff, d), lambda i: (layer, 0, 0)),
    ]
    args = (*xs, norm2, mods, mods, mods, w1, w2)
    if final:
        in_specs.append(pl.BlockSpec((1, d), lambda i: (0, 0)))
        args += (norm_f.reshape(1, d),)
    return pl.pallas_call(
        _mlp_final_kernel if final else functools.partial(_mlp_kernel, len(xs), rows.n_ctx // tm),
        grid=(n_out // tm,),
        in_specs=in_specs,
        out_specs=pl.BlockSpec((tm, d), lambda i: (i, 0)),
        out_shape=jax.ShapeDtypeStruct((n_out, d), F32),
        compiler_params=_cparams(("parallel",)),
        name="mlp_final" if final else "mlp",
    )(*args)


def kernel(x, c, ctx, c_ctx, ada_w, ada_b, norm1, norm2, norm_f, mlp_w1, mlp_w2, hgrn_w_in, hgrn_lb,
           hgrn_gnorm, hgrn_w_out, conv_w_in, conv_w, conv_b, conv_w_out):
    bsz, seq, d = x.shape
    ctx_len = ctx.shape[1]
    depth = ada_w.shape[0]
    assert bsz <= CTX_MOD_ROW and d % HEAD_DIM == 0
    assert seq % max(TM_MLP, TM_CONV, T_SCAN) == 0 and ctx_len % CHUNK == 0
    assert (bsz * ctx_len) % max(TM_MLP, TM_CONV, T_SCAN) == 0
    assert ctx_len & (ctx_len - 1) == 0 and TM_CONV % ctx_len == 0
    rows = _Rows(bsz, seq, ctx_len, d)

    c8 = jnp.concatenate([c, c_ctx[None, :], jnp.zeros((MOD_ROWS - bsz - 1, d), F32)], axis=0)
    mods = _ada_table(c8, ada_w, ada_b).reshape(depth * MOD_ROWS * N_MODS, 1, d)
    lb = _lower_bounds(hgrn_lb).reshape(2, -1, 1, d)

    norm1 = norm1.reshape(depth, 1, d)
    norm2 = norm2.reshape(depth, 1, d)
    gnorm = hgrn_gnorm.reshape(-1, 1, d)
    conv_b = conv_b.reshape(-1, 1, d)
    w1, w2 = mlp_w1.astype(BF16), mlp_w2.astype(BF16)
    hw_in, hw_out = _prep_weight_pieces(hgrn_w_in), hgrn_w_out.astype(BF16)
    cw_in, cw_out = conv_w_in.astype(BF16), conv_w_out.astype(BF16)

    xs = (ctx.reshape(bsz * ctx_len, d), x.reshape(bsz * seq, d))
    for i in range(depth):
        j = i // N_MIXERS
        if i % N_MIXERS == 0:
            packed, dec = _hgrn_prep(rows, xs, norm1, mods, hw_in, lb, i, j)
            o_bwd = _hgrn_scan_bwd(rows, packed, dec)
            xs = tuple(_hgrn_scan_fwd(rows, packed, dec, o_bwd, xs, gnorm, mods, hw_out, i, j))
        else:
            xs = (_conv_mixer(rows, xs[0], norm1, mods, cw_in, conv_w, conv_b, cw_out, i, j,
                              along_rows=(j % 2 == 1)),)
        xs = (_mlp(rows, xs, norm2, mods, w1, w2, i, norm_f if i == depth - 1 else None),)
    return xs[0].reshape(bsz, seq, d)
```

```python
import functools

import jax
import jax.numpy as jnp
from jax import lax
from jax.experimental import pallas as pl
from jax.experimental.pallas import tpu as pltpu

F32 = jnp.float32
BF16 = jnp.bfloat16

GRID_W = 64
HEAD_DIM = 128
CHUNK = 64
EPS = 1e-6
N_MIXERS = 2
CTX_MOD_ROW = 4
MOD_ROWS = 8
N_MODS = 6
DEC_ROWS = 8

VMEM_LIMIT = 56 * 1024 * 1024

TM_PREP = 512
PREP_PIECES = 4
PK_QS, PK_KS, PK_V, PK_GATE, PK_BLOCKS = 0, 1, 4, 5, 6
TM_MLP = 1024
FF_CHUNK = 1024
MLP_W_SLAB = 128
TM_CONV = 1024
T_SCAN = 512
TN_ADA = 2048


def _cparams(sem):
    return pltpu.CompilerParams(dimension_semantics=sem, vmem_limit_bytes=VMEM_LIMIT)


def _const_spec(shape, index_map):
    return pl.BlockSpec(shape, index_map, pipeline_mode=pl.Buffered(1))


def _silu(t):
    return t * jax.nn.sigmoid(t)


def _rms_mod(x, gain, shift, scale):
    y = x * lax.rsqrt(jnp.mean(x * x, axis=-1, keepdims=True) + EPS) * gain
    return y * (1.0 + scale) + shift


def _ada_kernel(c_ref, w_ref, b_ref, o_ref):
    s = _silu(c_ref[...]).astype(BF16)
    o_ref[...] = jnp.dot(s, w_ref[...].astype(BF16), preferred_element_type=F32) + b_ref[...]


def _ada_table(c8, ada_w, ada_b):
    depth, d, n = ada_w.shape
    return pl.pallas_call(
        _ada_kernel,
        grid=(depth, n // TN_ADA),
        in_specs=[
            pl.BlockSpec((MOD_ROWS, d), lambda l, j: (0, 0)),
            pl.BlockSpec((None, d, TN_ADA), lambda l, j: (l, 0, j)),
            pl.BlockSpec((None, 1, TN_ADA), lambda l, j: (l, 0, j)),
        ],
        out_specs=pl.BlockSpec((None, MOD_ROWS, TN_ADA), lambda l, j: (l, 0, j)),
        out_shape=jax.ShapeDtypeStruct((depth, MOD_ROWS, n), F32),
        compiler_params=_cparams(("parallel", "parallel")),
        name="ada_table",
    )(c8, ada_w, ada_b.reshape(depth, 1, n))


def _lb_kernel(x_ref, o_ref):
    n_rec = x_ref.shape[1]
    for d in range(x_ref.shape[0]):
        rows = [x_ref[d, j:j + 1, :] for j in range(n_rec)]
        m = functools.reduce(jnp.maximum, rows)
        e = [jnp.exp(r - m) for r in rows]
        tot = functools.reduce(lambda a, b: a + b, e)
        acc = None
        for j in range(n_rec):
            p = e[j] / tot
            acc = p if acc is None else acc + p
            o_ref[d, j:j + 1, :] = acc - e[0] / tot


def _lower_bounds(hgrn_lb):
    return pl.pallas_call(
        _lb_kernel,
        out_shape=jax.ShapeDtypeStruct(hgrn_lb.shape, F32),
        name="hgrn_lower_bounds",
    )(hgrn_lb)


class _Rows:
    def __init__(self, bsz, seq, ctx_len, d):
        self.bsz, self.seq, self.ctx_len, self.d = bsz, seq, ctx_len, d
        self.n_ctx = bsz * ctx_len
        self.n_lat = bsz * seq
        self.n_rows = self.n_ctx + self.n_lat

    def mod_row(self, tm):
        n_ctx_tiles = self.n_ctx // tm
        per_batch = self.seq // tm
        return lambda i: jnp.where(i < n_ctx_tiles, CTX_MOD_ROW, (i - n_ctx_tiles) // per_batch)


def _mod_block(d, layer, k, row_of_grid):
    return pl.BlockSpec((None, 1, d),
                        lambda *g: ((layer * MOD_ROWS + row_of_grid(*g)) * N_MODS + k, 0, 0))


def _layer_vec_spec(d, layer):
    return pl.BlockSpec((None, 1, d), lambda *g: (layer, 0, 0))


def _cumsum_rows(g, reverse):
    n = g.shape[0]
    row = lax.broadcasted_iota(jnp.int32, g.shape, 0)
    s = 1
    while s < n:
        if reverse:
            g = g + jnp.where(row < n - s, pltpu.roll(g, n - s, axis=0), 0.0)
        else:
            g = g + jnp.where(row >= s, pltpu.roll(g, s, axis=0), 0.0)
        s *= 2
    return g


def _split_sources(n_src, refs):
    return refs[:n_src], refs[n_src:]


def _load_rows(x_refs, is_ctx):
    if len(x_refs) == 1:
        return x_refs[0][...]
    return jnp.where(is_ctx, x_refs[0][...], x_refs[1][...])


def _prep_kernel(n_src, ctx_tiles, *refs):
    x_refs, refs = _split_sources(n_src, refs)
    (gain_ref, sh_ref, sc_ref, w_ref, lbf_ref, lbb_ref,
     pk_ref, dec_ref, za_ref, zb_ref, h_ref) = refs
    tm, d = h_ref.shape
    n_chunks = tm // CHUNK
    n_units = za_ref.shape[0]
    wpiece = za_ref.shape[-1]
    n_parts = n_units * wpiece // d
    ws = d // n_units
    step = pl.program_id(0)
    lb_refs = (lbf_ref, lbb_ref)
    assert n_parts == 5 and wpiece == n_parts * ws

    @pl.when(step == 0)
    def _():
        zb_ref[...] = jnp.zeros(zb_ref.shape, F32)

    def body(zw_ref, zr_ref):
        x = _load_rows(x_refs, step < ctx_tiles)
        h_ref[...] = _rms_mod(x, gain_ref[...], sh_ref[...], sc_ref[...]).astype(BF16)
        dec_ref[:, 4:, :] = jnp.zeros((n_chunks, DEC_ROWS - 4, d), F32)
        for ci in range(n_chunks):
            if ci % (n_chunks // n_units) == 0:
                it = ci // (n_chunks // n_units)
                zw_ref[it] = jnp.dot(h_ref[...], w_ref[it], preferred_element_type=F32)
            rows = slice(ci * CHUNK, (ci + 1) * CHUNK)
            for j in range(n_units):
                cols = slice(j * ws, (j + 1) * ws)

                def part(p):
                    piece, k = divmod(p * n_units + j, n_parts)
                    return zr_ref[piece, rows, k * ws:(k + 1) * ws]

                def put(block, val):
                    pk_ref[rows, block * d + j * ws:block * d + (j + 1) * ws] = val.astype(BF16)

                put(PK_V, part(2))
                put(PK_GATE, _silu(part(4)))
                q = _silu(part(3))
                for dirn, lb_ref in enumerate(lb_refs):
                    lb = lb_ref[:, cols]
                    f = lb + (1.0 - lb) * jax.nn.sigmoid(part(dirn))
                    cum = _cumsum_rows(jnp.log(f), dirn == 1)
                    if dirn == 1:
                        ref, last = cum[CHUNK // 2:CHUNK // 2 + 1], cum[0:1]
                    else:
                        ref, last = cum[CHUNK // 2 - 1:CHUNK // 2], cum[CHUNK - 1:CHUNK]
                    put(PK_QS + 2 * dirn, q * jnp.exp(cum - ref))
                    put(PK_KS + 2 * dirn, (1.0 - f) * jnp.exp(ref - cum))
                    dec_ref[ci, 2 * dirn:2 * dirn + 1, cols] = jnp.exp(ref)
                    dec_ref[ci, 2 * dirn + 1:2 * dirn + 2, cols] = jnp.exp(last - ref)

    @pl.when(step % 2 == 0)
    def _():
        body(za_ref, zb_ref)

    @pl.when(step % 2 == 1)
    def _():
        body(zb_ref, za_ref)


def _source_specs(rows, xs, tm, tile_of_grid):
    d = rows.d
    if len(xs) == 1:
        return [pl.BlockSpec((tm, d), lambda *g: (tile_of_grid(*g), 0))]
    n_ctx_tiles = rows.n_ctx // tm
    return [pl.BlockSpec((tm, d), lambda *g: (jnp.minimum(tile_of_grid(*g), n_ctx_tiles - 1), 0)),
            pl.BlockSpec((tm, d), lambda *g: (jnp.maximum(tile_of_grid(*g) - n_ctx_tiles, 0), 0))]


def _cast_kernel(w_ref, o_ref):
    o_ref[...] = w_ref[...].astype(o_ref.dtype)


def _prep_weight_pieces(w_in):
    n_layers, d, n = w_in.shape
    wp = n // PREP_PIECES
    return pl.pallas_call(
        _cast_kernel,
        grid=(n_layers, PREP_PIECES),
        in_specs=[pl.BlockSpec((None, d, wp), lambda l, i: (l, 0, i))],
        out_specs=pl.BlockSpec((None, None, d, wp), lambda l, i: (l, i, 0, 0)),
        out_shape=jax.ShapeDtypeStruct((n_layers, PREP_PIECES, d, wp), BF16),
        compiler_params=_cparams(("parallel", "parallel")),
        name="prep_weight_pieces",
    )(w_in)


def _hgrn_prep(rows, xs, norm1, mods, w_pieces, lb, layer, j):
    d = rows.d
    tm = TM_PREP
    n_pieces, _, piece_cols = w_pieces.shape[1:]
    n_tiles = rows.n_rows // tm
    cur = lambda s: jnp.minimum(s, n_tiles - 1)
    prev = lambda s: jnp.maximum(s - 1, 0)
    row = lambda s: rows.mod_row(tm)(cur(s))
    z_shape = pltpu.VMEM((n_pieces, tm, piece_cols), F32)
    return pl.pallas_call(
        functools.partial(_prep_kernel, len(xs), rows.n_ctx // tm),
        grid=(n_tiles + 1,),
        in_specs=_source_specs(rows, xs, tm, cur) + [
            _layer_vec_spec(d, layer),
            _mod_block(d, layer, 0, row),
            _mod_block(d, layer, 1, row),
            _const_spec((None, n_pieces, d, piece_cols), lambda s: (j, 0, 0, 0)),
            pl.BlockSpec((None, None, 1, d), lambda s: (0, j, 0, 0)),
            pl.BlockSpec((None, None, 1, d), lambda s: (1, j, 0, 0)),
        ],
        out_specs=[pl.BlockSpec((tm, PK_BLOCKS * d), lambda s: (prev(s), 0)),
                   pl.BlockSpec((tm // CHUNK, DEC_ROWS, d), lambda s: (prev(s), 0, 0))],
        out_shape=[jax.ShapeDtypeStruct((rows.n_rows, PK_BLOCKS * d), BF16),
                   jax.ShapeDtypeStruct((rows.n_rows // CHUNK, DEC_ROWS, d), F32)],
        scratch_shapes=[z_shape, z_shape, pltpu.VMEM((tm, d), BF16)],
        compiler_params=_cparams(("arbitrary",)),
        name="hgrn_prep",
    )(*xs, norm1, mods, mods, w_pieces, lb, lb)


def _gla_tile(qs_ref, ks_ref, v_ref, dec_ref, s_ref, dirn, emit):
    t_rows, d = qs_ref.shape
    reverse = dirn == 1
    n_chunks = t_rows // CHUNK
    r = lax.broadcasted_iota(jnp.int32, (CHUNK, CHUNK), 0)
    c = lax.broadcasted_iota(jnp.int32, (CHUNK, CHUNK), 1)
    mask = (c >= r) if reverse else (c <= r)
    nt = (((1,), (1,)), ((), ()))
    tn = (((0,), (0,)), ((), ()))
    order = range(n_chunks - 1, -1, -1) if reverse else range(n_chunks)
    heads = range(d // HEAD_DIM)
    hsl = lambda h: slice(h * HEAD_DIM, (h + 1) * HEAD_DIM)
    rsl = lambda ci: slice(ci * CHUNK, (ci + 1) * CHUNK)
    p, u = {}, {}
    for h in heads:
        for ci in order:
            sc = lax.dot_general(qs_ref[rsl(ci), hsl(h)], ks_ref[rsl(ci), hsl(h)], nt,
                                 preferred_element_type=F32)
            p[h, ci] = jnp.where(mask, sc, 0.0).astype(BF16)
    for h in heads:
        for ci in order:
            u[h, ci] = lax.dot_general(v_ref[rsl(ci), hsl(h)], ks_ref[rsl(ci), hsl(h)], tn,
                                       preferred_element_type=F32)
    for h in heads:
        st = s_ref[h]
        for ci in order:
            e_ref = dec_ref[ci, 2 * dirn:2 * dirn + 1, hsl(h)]
            e_lr = dec_ref[ci, 2 * dirn + 1:2 * dirn + 2, hsl(h)]
            o = lax.dot_general(qs_ref[rsl(ci), hsl(h)], (st * e_ref).astype(BF16), nt,
                                preferred_element_type=F32)
            o = o + jnp.dot(p[h, ci], v_ref[rsl(ci), hsl(h)], preferred_element_type=F32)
            st = st * (e_ref * e_lr) + u[h, ci] * e_lr
            emit(rsl(ci), hsl(h), o)
        s_ref[h] = st


def _scan_bwd_kernel(qc, kc, vc, dc, ql, kl, vl, dl, oc_ref, ol_ref, s_ref):
    def run(qs_ref, ks_ref, v_ref, dec_ref, o_ref):
        def emit(rows, hs, o):
            o_ref[rows, hs] = o

        _gla_tile(qs_ref, ks_ref, v_ref, dec_ref, s_ref, 1, emit)

    @pl.when(pl.program_id(1) == 0)
    def _():
        s_ref[...] = jnp.zeros_like(s_ref)
        run(qc, kc, vc, dc, oc_ref)

    @pl.when(pl.program_id(1) > 0)
    def _():
        run(ql, kl, vl, dl, ol_ref)


def _scan_fwd_kernel(qc, kc, vc, dc, gc, obc, xc, ql, kl, vl, dl, gl, obl, xl,
                     gn_ref, g1_ref, wout_ref, oc_ref, ol_ref, s_ref, y_ref):
    def run(qs_ref, ks_ref, v_ref, dec_ref, gate_ref, ob_ref, x_ref, o_ref):
        n = qs_ref.shape[0]

        def emit(rows, hs, o):
            o = o + ob_ref[rows, hs]
            o = o * lax.rsqrt(jnp.mean(o * o, axis=-1, keepdims=True) + EPS)
            y_ref[rows, hs] = (o * gn_ref[:, hs] * gate_ref[rows, hs].astype(F32)).astype(BF16)

        _gla_tile(qs_ref, ks_ref, v_ref, dec_ref, s_ref, 0, emit)
        y = jnp.dot(y_ref[0:n, :], wout_ref[...], preferred_element_type=F32)
        o_ref[...] = x_ref[...] + g1_ref[...] * y

    @pl.when(pl.program_id(1) == 0)
    def _():
        s_ref[...] = jnp.zeros_like(s_ref)
        run(qc, kc, vc, dc, gc, obc, xc, oc_ref)

    @pl.when(pl.program_id(1) > 0)
    def _():
        run(ql, kl, vl, dl, gl, obl, xl, ol_ref)


class _ScanBlocks:
    def __init__(self, rows, reverse):
        self.rows, self.t = rows, T_SCAN
        self.n_lat = rows.seq // self.t
        self.steps = 1 + self.n_lat
        n_lat = self.n_lat
        if reverse:
            self.lat = lambda b, s: b * n_lat + (n_lat - 1 - jnp.maximum(s - 1, 0))
        else:
            self.lat = lambda b, s: b * n_lat + jnp.maximum(s - 1, 0)
        self.flat_lat0 = rows.n_ctx // self.t

    def flat(self, width, col=0):
        r, t, off, lat = self.rows, self.t, self.flat_lat0, self.lat
        return [pl.BlockSpec((r.ctx_len, width), lambda b, s: (b, col)),
                pl.BlockSpec((t, width), lambda b, s: (off + lat(b, s), col))]

    def split(self, width):
        r, t, lat = self.rows, self.t, self.lat
        return [pl.BlockSpec((r.ctx_len, width), lambda b, s: (b, 0)),
                pl.BlockSpec((t, width), lambda b, s: (lat(b, s), 0))]

    def dec(self):
        r, t, off, lat = self.rows, self.t, self.flat_lat0, self.lat
        d = r.d
        return [pl.BlockSpec((r.ctx_len // CHUNK, DEC_ROWS, d), lambda b, s: (b, 0, 0)),
                pl.BlockSpec((t // CHUNK, DEC_ROWS, d), lambda b, s: (off + lat(b, s), 0, 0))]

    def out_shapes(self, dtype):
        r = self.rows
        return [jax.ShapeDtypeStruct((r.n_ctx, r.d), dtype), jax.ShapeDtypeStruct((r.n_lat, r.d), dtype)]


def _interleave(*pairs):
    return [p[0] for p in pairs] + [p[1] for p in pairs]


def _hgrn_scan_bwd(rows, packed, dec):
    d = rows.d
    blk = _ScanBlocks(rows, True)
    pk = lambda k: blk.flat(d, k)
    return pl.pallas_call(
        _scan_bwd_kernel,
        grid=(rows.bsz, blk.steps),
        in_specs=_interleave(pk(PK_QS + 2), pk(PK_KS + 2), pk(PK_V), blk.dec()),
        out_specs=blk.split(d),
        out_shape=blk.out_shapes(F32),
        scratch_shapes=[pltpu.VMEM((d // HEAD_DIM, HEAD_DIM, HEAD_DIM), F32)],
        compiler_params=_cparams(("arbitrary", "arbitrary")),
        name="hgrn_scan_bwd",
    )(*([packed, packed, packed, dec] * 2))


def _hgrn_scan_fwd(rows, packed, dec, o_bwd, xs, gnorm, mods, w_out, layer, j):
    d = rows.d
    blk = _ScanBlocks(rows, False)
    pk = lambda k: blk.flat(d, k)
    x_pair = blk.split(d) if len(xs) == 2 else blk.flat(d)
    x_args = list(xs) if len(xs) == 2 else [xs[0], xs[0]]
    mod_row = lambda b, s: jnp.where(s == 0, CTX_MOD_ROW, b)
    pairs = _interleave(pk(PK_QS), pk(PK_KS), pk(PK_V), blk.dec(), pk(PK_GATE), blk.split(d), x_pair)
    half = [packed, packed, packed, dec, packed]
    return pl.pallas_call(
        _scan_fwd_kernel,
        grid=(rows.bsz, blk.steps),
        in_specs=pairs + [_layer_vec_spec(d, j),
                          _mod_block(d, layer, 2, mod_row),
                          _const_spec((None, d, d), lambda b, s: (j, 0, 0))],
        out_specs=blk.split(d),
        out_shape=blk.out_shapes(F32),
        scratch_shapes=[pltpu.VMEM((d // HEAD_DIM, HEAD_DIM, HEAD_DIM), F32),
                        pltpu.VMEM((max(blk.t, rows.ctx_len), d), BF16)],
        compiler_params=_cparams(("arbitrary", "arbitrary")),
        name="hgrn_scan_fwd",
    )(*half, o_bwd[0], x_args[0], *half, o_bwd[1], x_args[1], gnorm, mods, w_out)


def _conv_taps(cw_ref, cb_ref, um, u, up):
    return cb_ref[...] + cw_ref[0:1, :] * um + cw_ref[1:2, :] * u + cw_ref[2:3, :] * up


def _conv_seq_kernel(n_ctx_tiles, ctx_len, x_ref, gain_ref, sh_ref, sc_ref, g1_ref, win_ref,
                     cw_ref, cb_ref, wout_ref, o_ref):
    x = x_ref[...]
    tm, d = x.shape
    h = _rms_mod(x, gain_ref[...], sh_ref[...], sc_ref[...]).astype(BF16)
    p = jnp.dot(h, win_ref[...], preferred_element_type=F32)
    gate_b, u = p[:, :d], p[:, d:2 * d] * p[:, 2 * d:]
    period = jnp.where(pl.program_id(0) < n_ctx_tiles, ctx_len, GRID_W)
    pos = lax.broadcasted_iota(jnp.int32, (tm, 1), 0) & (period - 1)
    um = jnp.where(pos == 0, 0.0, pltpu.roll(u, 1, axis=0))
    up = jnp.where(pos == period - 1, 0.0, pltpu.roll(u, tm - 1, axis=0))
    cv = _conv_taps(cw_ref, cb_ref, um, u, up)
    y = jnp.dot((gate_b * cv).astype(BF16), wout_ref[...], preferred_element_type=F32)
    o_ref[...] = x + g1_ref[...] * y


def _conv_rows_kernel(n_ctx_tiles, tiles_per_batch, xp_ref, x_ref, xn_ref, gain_ref, sh_ref,
                      sc_ref, g1_ref, win_ref, cw_ref, cb_ref, wout_ref, o_ref, h_ref):
    x = x_ref[...]
    tm, d = x.shape
    gain, sh, sc = gain_ref[...], sh_ref[...], sc_ref[...]
    h_ref[0:GRID_W, :] = _rms_mod(xp_ref[...], gain, sh, sc).astype(BF16)
    h_ref[GRID_W:GRID_W + tm, :] = _rms_mod(x, gain, sh, sc).astype(BF16)
    h_ref[GRID_W + tm:, :] = _rms_mod(xn_ref[...], gain, sh, sc).astype(BF16)
    p = jnp.dot(h_ref[...], win_ref[:, d:], preferred_element_type=F32)
    u = p[:, :d] * p[:, d:]
    gate_b = jnp.dot(h_ref[GRID_W:GRID_W + tm, :], win_ref[:, :d], preferred_element_type=F32)
    i = pl.program_id(0)
    jl = (i - n_ctx_tiles) % tiles_per_batch
    is_lat = i >= n_ctx_tiles
    has_prev = jnp.logical_and(is_lat, jl != 0)
    has_next = jnp.logical_and(is_lat, jl != tiles_per_batch - 1)
    row = lax.broadcasted_iota(jnp.int32, (tm, 1), 0)
    um = jnp.where(jnp.logical_or(row >= GRID_W, has_prev), u[0:tm], 0.0)
    up = jnp.where(jnp.logical_or(row < tm - GRID_W, has_next), u[2 * GRID_W:], 0.0)
    cv = _conv_taps(cw_ref, cb_ref, um, u[GRID_W:GRID_W + tm], up)
    y = jnp.dot((gate_b * cv).astype(BF16), wout_ref[...], preferred_element_type=F32)
    o_ref[...] = x + g1_ref[...] * y


def _conv_mixer(rows, x2, norm1, mods, w_in, cw, cb, w_out, layer, j, along_rows):
    d = rows.d
    tm = TM_CONV
    row = rows.mod_row(tm)
    n_tiles = rows.n_rows // tm
    n_ctx_tiles = rows.n_ctx // tm
    common = [
        _layer_vec_spec(d, layer),
        _mod_block(d, layer, 0, row),
        _mod_block(d, layer, 1, row),
        _mod_block(d, layer, 2, row),
        _const_spec((None, d, 3 * d), lambda i: (j, 0, 0)),
        pl.BlockSpec((None, 3, d), lambda i: (j, 0, 0)),
        _layer_vec_spec(d, j),
        _const_spec((None, d, d), lambda i: (j, 0, 0)),
    ]
    args = (norm1, mods, mods, mods, w_in, cw, cb, w_out)
    if along_rows:
        sub = tm // GRID_W
        last = rows.n_rows // GRID_W - 1
        kern = functools.partial(_conv_rows_kernel, n_ctx_tiles, rows.seq // tm)
        in_specs = [pl.BlockSpec((GRID_W, d), lambda i: (jnp.maximum(i * sub - 1, 0), 0)),
                    pl.BlockSpec((tm, d), lambda i: (i, 0)),
                    pl.BlockSpec((GRID_W, d), lambda i: (jnp.minimum((i + 1) * sub, last), 0))]
        args = (x2, x2, x2) + args
        scratch = [pltpu.VMEM((tm + 2 * GRID_W, d), BF16)]
        name = "conv_mixer_rows"
    else:
        kern = functools.partial(_conv_seq_kernel, n_ctx_tiles, rows.ctx_len)
        in_specs = [pl.BlockSpec((tm, d), lambda i: (i, 0))]
        args = (x2,) + args
        scratch = []
        name = "conv_mixer_seq"
    return pl.pallas_call(
        kern,
        grid=(n_tiles,),
        in_specs=in_specs + common,
        out_specs=pl.BlockSpec((tm, d), lambda i: (i, 0)),
        out_shape=jax.ShapeDtypeStruct((rows.n_rows, d), F32),
        scratch_shapes=scratch,
        compiler_params=_cparams(("parallel",)),
        name=name,
    )(*args)


def _mlp_residual(x, gain_ref, sh_ref, sc_ref, g2_ref, w1_ref, w2_ref):
    h = _rms_mod(x, gain_ref[...], sh_ref[...], sc_ref[...]).astype(BF16)
    d_ff = w1_ref.shape[-1]
    y = None
    for lo in range(0, d_ff, FF_CHUNK):
        a = jnp.dot(h, w1_ref[:, lo:lo + FF_CHUNK], preferred_element_type=F32)
        a = jnp.square(jnp.maximum(a, 0.0)).astype(BF16)
        part = jnp.dot(a, w2_ref[lo:lo + FF_CHUNK, :], preferred_element_type=F32)
        y = part if y is None else y + part
    return x + g2_ref[...] * y


def _mlp_kernel(n_src, ctx_tiles, *refs):
    x_refs, refs = _split_sources(n_src, refs)
    (gain_ref, sh_ref, sc_ref, g2_ref, w1_ref, w2_ref, nw1_ref, nw2_ref,
     o_ref, ow1_ref, ow2_ref) = refs
    x = _load_rows(x_refs, pl.program_id(0) < ctx_tiles)
    o_ref[...] = _mlp_residual(x, gain_ref, sh_ref, sc_ref, g2_ref, w1_ref, w2_ref)
    ow1_ref[...] = nw1_ref[...].astype(BF16)
    ow2_ref[...] = nw2_ref[...].astype(BF16)


def _mlp_final_kernel(x_ref, gain_ref, sh_ref, sc_ref, g2_ref, w1_ref, w2_ref, gf_ref, o_ref):
    x = _mlp_residual(x_ref[...], gain_ref, sh_ref, sc_ref, g2_ref, w1_ref, w2_ref)
    o_ref[...] = x * lax.rsqrt(jnp.mean(x * x, axis=-1, keepdims=True) + EPS) * gf_ref[...]


def _mlp(rows, xs, norm2, mods, w1, w2, layer, next_w=None, norm_f=None):
    d = rows.d
    d_ff = w1.shape[-1]
    tm = TM_MLP
    final = norm_f is not None
    assert final != (next_w is not None) and not (final and len(xs) != 1)
    first = rows.n_ctx // tm if final else 0
    n_out = rows.n_lat if final else rows.n_rows
    n_steps = n_out // tm
    row = lambda i: rows.mod_row(tm)(i + first)
    in_specs = _source_specs(rows, xs, tm, lambda i: i + first) + [
        _layer_vec_spec(d, layer),
        _mod_block(d, layer, 3, row),
        _mod_block(d, layer, 4, row),
        _mod_block(d, layer, 5, row),
        _const_spec((d, d_ff), lambda i: (0, 0)),
        _const_spec((d_ff, d), lambda i: (0, 0)),
    ]
    args = (*xs, norm2, mods, mods, mods, w1, w2)
    out_specs = [pl.BlockSpec((tm, d), lambda i: (i, 0))]
    out_shape = [jax.ShapeDtypeStruct((n_out, d), F32)]
    if final:
        in_specs.append(pl.BlockSpec((1, d), lambda i: (0, 0)))
        args += (norm_f.reshape(1, d),)
        kern = _mlp_final_kernel
    else:
        n_slabs = d_ff // MLP_W_SLAB
        assert n_slabs <= n_steps
        slab = lambda i: jnp.minimum(i, n_slabs - 1)
        in_specs += [pl.BlockSpec((None, d, MLP_W_SLAB), lambda i: (layer + 1, 0, slab(i))),
                     pl.BlockSpec((None, MLP_W_SLAB, d), lambda i: (layer + 1, slab(i), 0))]
        args += tuple(next_w)
        out_specs += [pl.BlockSpec((d, MLP_W_SLAB), lambda i: (0, slab(i))),
                      pl.BlockSpec((MLP_W_SLAB, d), lambda i: (slab(i), 0))]
        out_shape += [jax.ShapeDtypeStruct((d, d_ff), BF16), jax.ShapeDtypeStruct((d_ff, d), BF16)]
        kern = functools.partial(_mlp_kernel, len(xs), rows.n_ctx // tm)
    res = pl.pallas_call(
        kern,
        grid=(n_steps,),
        in_specs=in_specs,
        out_specs=out_specs,
        out_shape=out_shape,
        compiler_params=_cparams(("arbitrary",)),
        name="mlp_final" if final else "mlp",
    )(*args)
    return res[0] if final else res


def kernel(x, c, ctx, c_ctx, ada_w, ada_b, norm1, norm2, norm_f, mlp_w1, mlp_w2, hgrn_w_in, hgrn_lb,
           hgrn_gnorm, hgrn_w_out, conv_w_in, conv_w, conv_b, conv_w_out):
    bsz, seq, d = x.shape
    ctx_len = ctx.shape[1]
    depth = ada_w.shape[0]
    assert bsz <= CTX_MOD_ROW and d % HEAD_DIM == 0
    assert seq % max(TM_MLP, TM_CONV, T_SCAN) == 0 and ctx_len % CHUNK == 0
    assert (bsz * ctx_len) % max(TM_MLP, TM_CONV, T_SCAN) == 0
    assert ctx_len & (ctx_len - 1) == 0 and TM_CONV % ctx_len == 0
    rows = _Rows(bsz, seq, ctx_len, d)

    c8 = jnp.concatenate([c, c_ctx[None, :], jnp.zeros((MOD_ROWS - bsz - 1, d), F32)], axis=0)
    mods = _ada_table(c8, ada_w, ada_b).reshape(depth * MOD_ROWS * N_MODS, 1, d)
    lb = _lower_bounds(hgrn_lb).reshape(2, -1, 1, d)

    norm1 = norm1.reshape(depth, 1, d)
    norm2 = norm2.reshape(depth, 1, d)
    gnorm = hgrn_gnorm.reshape(-1, 1, d)
    conv_b = conv_b.reshape(-1, 1, d)
    w1, w2 = mlp_w1[0].astype(BF16), mlp_w2[0].astype(BF16)
    hw_in, hw_out = _prep_weight_pieces(hgrn_w_in), hgrn_w_out.astype(BF16)
    cw_in, cw_out = conv_w_in.astype(BF16), conv_w_out.astype(BF16)

    xs = (ctx.reshape(bsz * ctx_len, d), x.reshape(bsz * seq, d))
    for i in range(depth):
        j = i // N_MIXERS
        if i % N_MIXERS == 0:
            packed, dec = _hgrn_prep(rows, xs, norm1, mods, hw_in, lb, i, j)
            o_bwd = _hgrn_scan_bwd(rows, packed, dec)
            xs = tuple(_hgrn_scan_fwd(rows, packed, dec, o_bwd, xs, gnorm, mods, hw_out, i, j))
        else:
            xs = (_conv_mixer(rows, xs[0], norm1, mods, cw_in, conv_w, conv_b, cw_out, i, j,
                              along_rows=(j % 2 == 1)),)
        if i < depth - 1:
            x2, w1, w2 = _mlp(rows, xs, norm2, mods, w1, w2, i, next_w=(mlp_w1, mlp_w2))
        else:
            x2 = _mlp(rows, xs, norm2, mods, w1, w2, i, norm_f=norm_f)
        xs = (x2,)
    return xs[0].reshape(bsz, seq, d)
```

```python
import functools

import jax
import jax.numpy as jnp
from jax import lax
from jax.experimental import pallas as pl
from jax.experimental.pallas import tpu as pltpu

F32 = jnp.float32
BF16 = jnp.bfloat16

GRID_W = 64
HEAD_DIM = 128
CHUNK = 64
EPS = 1e-6
N_MIXERS = 2
CTX_MOD_ROW = 4
MOD_ROWS = 8
N_MODS = 6
DEC_ROWS = 8

VMEM_LIMIT = 56 * 1024 * 1024

TM_PREP = 512
PREP_PIECES = 4
PK_QS, PK_KS, PK_V, PK_GATE, PK_BLOCKS = 0, 1, 4, 5, 6
TM_MLP = 1024
FF_CHUNK = 1024
MLP_W_SLAB = 128
TM_CONV = 1024
T_SCAN = 512
TN_ADA = 2048


def _cparams(sem):
    return pltpu.CompilerParams(dimension_semantics=sem, vmem_limit_bytes=VMEM_LIMIT)


def _const_spec(shape, index_map):
    return pl.BlockSpec(shape, index_map, pipeline_mode=pl.Buffered(1))


def _silu(t):
    return t * jax.nn.sigmoid(t)


def _rms_mod(x, gain, shift, scale):
    y = x * lax.rsqrt(jnp.mean(x * x, axis=-1, keepdims=True) + EPS) * gain
    return y * (1.0 + scale) + shift


def _ada_kernel(c_ref, w_ref, b_ref, o_ref):
    s = _silu(c_ref[...]).astype(BF16)
    o_ref[...] = jnp.dot(s, w_ref[...].astype(BF16), preferred_element_type=F32) + b_ref[...]


def _ada_table(c8, ada_w, ada_b):
    depth, d, n = ada_w.shape
    return pl.pallas_call(
        _ada_kernel,
        grid=(depth, n // TN_ADA),
        in_specs=[
            pl.BlockSpec((MOD_ROWS, d), lambda l, j: (0, 0)),
            pl.BlockSpec((None, d, TN_ADA), lambda l, j: (l, 0, j)),
            pl.BlockSpec((None, 1, TN_ADA), lambda l, j: (l, 0, j)),
        ],
        out_specs=pl.BlockSpec((None, MOD_ROWS, TN_ADA), lambda l, j: (l, 0, j)),
        out_shape=jax.ShapeDtypeStruct((depth, MOD_ROWS, n), F32),
        compiler_params=_cparams(("parallel", "parallel")),
        name="ada_table",
    )(c8, ada_w, ada_b.reshape(depth, 1, n))


def _lb_kernel(x_ref, o_ref):
    n_rec = x_ref.shape[1]
    for d in range(x_ref.shape[0]):
        rows = [x_ref[d, j:j + 1, :] for j in range(n_rec)]
        m = functools.reduce(jnp.maximum, rows)
        e = [jnp.exp(r - m) for r in rows]
        tot = functools.reduce(lambda a, b: a + b, e)
        acc = None
        for j in range(n_rec):
            p = e[j] / tot
            acc = p if acc is None else acc + p
            o_ref[d, j:j + 1, :] = acc - e[0] / tot


def _lower_bounds(hgrn_lb):
    return pl.pallas_call(
        _lb_kernel,
        out_shape=jax.ShapeDtypeStruct(hgrn_lb.shape, F32),
        name="hgrn_lower_bounds",
    )(hgrn_lb)


class _Rows:
    def __init__(self, bsz, seq, ctx_len, d):
        self.bsz, self.seq, self.ctx_len, self.d = bsz, seq, ctx_len, d
        self.n_ctx = bsz * ctx_len
        self.n_lat = bsz * seq
        self.n_rows = self.n_ctx + self.n_lat

    def mod_row(self, tm):
        n_ctx_tiles = self.n_ctx // tm
        per_batch = self.seq // tm
        return lambda i: jnp.where(i < n_ctx_tiles, CTX_MOD_ROW, (i - n_ctx_tiles) // per_batch)


def _mod_block(d, layer, k, row_of_grid):
    return pl.BlockSpec((None, 1, d),
                        lambda *g: ((layer * MOD_ROWS + row_of_grid(*g)) * N_MODS + k, 0, 0))


def _layer_vec_spec(d, layer):
    return pl.BlockSpec((None, 1, d), lambda *g: (layer, 0, 0))


def _cumsum_rows(g, reverse):
    n = g.shape[0]
    row = lax.broadcasted_iota(jnp.int32, g.shape, 0)
    s = 1
    while s < n:
        if reverse:
            g = g + jnp.where(row < n - s, pltpu.roll(g, n - s, axis=0), 0.0)
        else:
            g = g + jnp.where(row >= s, pltpu.roll(g, s, axis=0), 0.0)
        s *= 2
    return g


def _split_sources(n_src, refs):
    return refs[:n_src], refs[n_src:]


def _load_rows(x_refs, is_ctx):
    if len(x_refs) == 1:
        return x_refs[0][...]
    return jnp.where(is_ctx, x_refs[0][...], x_refs[1][...])


def _prep_kernel(n_src, ctx_tiles, *refs):
    x_refs, refs = _split_sources(n_src, refs)
    (gain_ref, sh_ref, sc_ref, w_ref, lbf_ref, lbb_ref,
     pk_ref, dec_ref, za_ref, zb_ref, h_ref) = refs
    tm, d = h_ref.shape
    n_chunks = tm // CHUNK
    n_units = za_ref.shape[0]
    wpiece = za_ref.shape[-1]
    n_parts = n_units * wpiece // d
    ws = d // n_units
    step = pl.program_id(0)
    lb_refs = (lbf_ref, lbb_ref)
    assert n_parts == 5 and wpiece == n_parts * ws

    @pl.when(step == 0)
    def _():
        zb_ref[...] = jnp.zeros(zb_ref.shape, F32)

    def body(zw_ref, zr_ref):
        x = _load_rows(x_refs, step < ctx_tiles)
        h_ref[...] = _rms_mod(x, gain_ref[...], sh_ref[...], sc_ref[...]).astype(BF16)
        dec_ref[:, 4:, :] = jnp.zeros((n_chunks, DEC_ROWS - 4, d), F32)
        for ci in range(n_chunks):
            if ci % (n_chunks // n_units) == 0:
                it = ci // (n_chunks // n_units)
                zw_ref[it] = jnp.dot(h_ref[...], w_ref[it], preferred_element_type=F32)
            rows = slice(ci * CHUNK, (ci + 1) * CHUNK)
            for j in range(n_units):
                cols = slice(j * ws, (j + 1) * ws)

                def part(p):
                    piece, k = divmod(p * n_units + j, n_parts)
                    return zr_ref[piece, rows, k * ws:(k + 1) * ws]

                def put(block, val):
                    pk_ref[rows, block * d + j * ws:block * d + (j + 1) * ws] = val.astype(BF16)

                put(PK_V, part(2))
                put(PK_GATE, _silu(part(4)))
                q = _silu(part(3))
                for dirn, lb_ref in enumerate(lb_refs):
                    lb = lb_ref[:, cols]
                    f = lb + (1.0 - lb) * jax.nn.sigmoid(part(dirn))
                    cum = _cumsum_rows(jnp.log(f), dirn == 1)
                    if dirn == 1:
                        ref, last = cum[CHUNK // 2:CHUNK // 2 + 1], cum[0:1]
                    else:
                        ref, last = cum[CHUNK // 2 - 1:CHUNK // 2], cum[CHUNK - 1:CHUNK]
                    e_rel = jnp.exp(cum - ref)
                    put(PK_QS + 2 * dirn, q * e_rel)
                    put(PK_KS + 2 * dirn, (1.0 - f) / e_rel)
                    dec_ref[ci, 2 * dirn:2 * dirn + 1, cols] = jnp.exp(ref)
                    dec_ref[ci, 2 * dirn + 1:2 * dirn + 2, cols] = jnp.exp(last - ref)

    @pl.when(step % 2 == 0)
    def _():
        body(za_ref, zb_ref)

    @pl.when(step % 2 == 1)
    def _():
        body(zb_ref, za_ref)


def _source_specs(rows, xs, tm, tile_of_grid):
    d = rows.d
    if len(xs) == 1:
        return [pl.BlockSpec((tm, d), lambda *g: (tile_of_grid(*g), 0))]
    n_ctx_tiles = rows.n_ctx // tm
    return [pl.BlockSpec((tm, d), lambda *g: (jnp.minimum(tile_of_grid(*g), n_ctx_tiles - 1), 0)),
            pl.BlockSpec((tm, d), lambda *g: (jnp.maximum(tile_of_grid(*g) - n_ctx_tiles, 0), 0))]


def _cast_kernel(w_ref, o_ref):
    o_ref[...] = w_ref[...].astype(o_ref.dtype)


def _prep_weight_pieces(w_in):
    n_layers, d, n = w_in.shape
    wp = n // PREP_PIECES
    return pl.pallas_call(
        _cast_kernel,
        grid=(n_layers, PREP_PIECES),
        in_specs=[pl.BlockSpec((None, d, wp), lambda l, i: (l, 0, i))],
        out_specs=pl.BlockSpec((None, None, d, wp), lambda l, i: (l, i, 0, 0)),
        out_shape=jax.ShapeDtypeStruct((n_layers, PREP_PIECES, d, wp), BF16),
        compiler_params=_cparams(("parallel", "parallel")),
        name="prep_weight_pieces",
    )(w_in)


def _hgrn_prep(rows, xs, norm1, mods, w_pieces, lb, layer, j):
    d = rows.d
    tm = TM_PREP
    n_pieces, _, piece_cols = w_pieces.shape[1:]
    n_tiles = rows.n_rows // tm
    cur = lambda s: jnp.minimum(s, n_tiles - 1)
    prev = lambda s: jnp.maximum(s - 1, 0)
    row = lambda s: rows.mod_row(tm)(cur(s))
    z_shape = pltpu.VMEM((n_pieces, tm, piece_cols), F32)
    return pl.pallas_call(
        functools.partial(_prep_kernel, len(xs), rows.n_ctx // tm),
        grid=(n_tiles + 1,),
        in_specs=_source_specs(rows, xs, tm, cur) + [
            _layer_vec_spec(d, layer),
            _mod_block(d, layer, 0, row),
            _mod_block(d, layer, 1, row),
            _const_spec((None, n_pieces, d, piece_cols), lambda s: (j, 0, 0, 0)),
            pl.BlockSpec((None, None, 1, d), lambda s: (0, j, 0, 0)),
            pl.BlockSpec((None, None, 1, d), lambda s: (1, j, 0, 0)),
        ],
        out_specs=[pl.BlockSpec((tm, PK_BLOCKS * d), lambda s: (prev(s), 0)),
                   pl.BlockSpec((tm // CHUNK, DEC_ROWS, d), lambda s: (prev(s), 0, 0))],
        out_shape=[jax.ShapeDtypeStruct((rows.n_rows, PK_BLOCKS * d), BF16),
                   jax.ShapeDtypeStruct((rows.n_rows // CHUNK, DEC_ROWS, d), F32)],
        scratch_shapes=[z_shape, z_shape, pltpu.VMEM((tm, d), BF16)],
        compiler_params=_cparams(("arbitrary",)),
        name="hgrn_prep",
    )(*xs, norm1, mods, mods, w_pieces, lb, lb)


def _gla_tile(qs_ref, ks_ref, v_ref, dec_ref, s_ref, dirn, emit):
    t_rows, d = qs_ref.shape
    reverse = dirn == 1
    n_chunks = t_rows // CHUNK
    r = lax.broadcasted_iota(jnp.int32, (CHUNK, CHUNK), 0)
    c = lax.broadcasted_iota(jnp.int32, (CHUNK, CHUNK), 1)
    mask = (c >= r) if reverse else (c <= r)
    nt = (((1,), (1,)), ((), ()))
    tn = (((0,), (0,)), ((), ()))
    order = range(n_chunks - 1, -1, -1) if reverse else range(n_chunks)
    heads = range(d // HEAD_DIM)
    hsl = lambda h: slice(h * HEAD_DIM, (h + 1) * HEAD_DIM)
    rsl = lambda ci: slice(ci * CHUNK, (ci + 1) * CHUNK)
    p, u = {}, {}
    for h in heads:
        for ci in order:
            sc = lax.dot_general(qs_ref[rsl(ci), hsl(h)], ks_ref[rsl(ci), hsl(h)], nt,
                                 preferred_element_type=F32)
            p[h, ci] = jnp.where(mask, sc, 0.0).astype(BF16)
    for h in heads:
        for ci in order:
            u[h, ci] = lax.dot_general(v_ref[rsl(ci), hsl(h)], ks_ref[rsl(ci), hsl(h)], tn,
                                       preferred_element_type=F32)
    for h in heads:
        st = s_ref[h]
        for ci in order:
            e_ref = dec_ref[ci, 2 * dirn:2 * dirn + 1, hsl(h)]
            e_lr = dec_ref[ci, 2 * dirn + 1:2 * dirn + 2, hsl(h)]
            o = lax.dot_general(qs_ref[rsl(ci), hsl(h)], (st * e_ref).astype(BF16), nt,
                                preferred_element_type=F32)
            o = o + jnp.dot(p[h, ci], v_ref[rsl(ci), hsl(h)], preferred_element_type=F32)
            st = st * (e_ref * e_lr) + u[h, ci] * e_lr
            emit(rsl(ci), hsl(h), o)
        s_ref[h] = st


def _scan_bwd_kernel(qc, kc, vc, dc, ql, kl, vl, dl, oc_ref, ol_ref, s_ref):
    def run(qs_ref, ks_ref, v_ref, dec_ref, o_ref):
        def emit(rows, hs, o):
            o_ref[rows, hs] = o

        _gla_tile(qs_ref, ks_ref, v_ref, dec_ref, s_ref, 1, emit)

    @pl.when(pl.program_id(1) == 0)
    def _():
        s_ref[...] = jnp.zeros_like(s_ref)
        run(qc, kc, vc, dc, oc_ref)

    @pl.when(pl.program_id(1) > 0)
    def _():
        run(ql, kl, vl, dl, ol_ref)


def _scan_fwd_kernel(qc, kc, vc, dc, gc, obc, xc, ql, kl, vl, dl, gl, obl, xl,
                     gn_ref, g1_ref, wout_ref, oc_ref, ol_ref, s_ref, y_ref):
    def run(qs_ref, ks_ref, v_ref, dec_ref, gate_ref, ob_ref, x_ref, o_ref):
        n = qs_ref.shape[0]

        def emit(rows, hs, o):
            o = o + ob_ref[rows, hs]
            o = o * lax.rsqrt(jnp.mean(o * o, axis=-1, keepdims=True) + EPS)
            y_ref[rows, hs] = (o * gn_ref[:, hs] * gate_ref[rows, hs].astype(F32)).astype(BF16)

        _gla_tile(qs_ref, ks_ref, v_ref, dec_ref, s_ref, 0, emit)
        y = jnp.dot(y_ref[0:n, :], wout_ref[...], preferred_element_type=F32)
        o_ref[...] = x_ref[...] + g1_ref[...] * y

    @pl.when(pl.program_id(1) == 0)
    def _():
        s_ref[...] = jnp.zeros_like(s_ref)
        run(qc, kc, vc, dc, gc, obc, xc, oc_ref)

    @pl.when(pl.program_id(1) > 0)
    def _():
        run(ql, kl, vl, dl, gl, obl, xl, ol_ref)


class _ScanBlocks:
    def __init__(self, rows, reverse):
        self.rows, self.t = rows, T_SCAN
        self.n_lat = rows.seq // self.t
        self.steps = 1 + self.n_lat
        n_lat = self.n_lat
        if reverse:
            self.lat = lambda b, s: b * n_lat + (n_lat - 1 - jnp.maximum(s - 1, 0))
        else:
            self.lat = lambda b, s: b * n_lat + jnp.maximum(s - 1, 0)
        self.flat_lat0 = rows.n_ctx // self.t

    def flat(self, width, col=0):
        r, t, off, lat = self.rows, self.t, self.flat_lat0, self.lat
        return [pl.BlockSpec((r.ctx_len, width), lambda b, s: (b, col)),
                pl.BlockSpec((t, width), lambda b, s: (off + lat(b, s), col))]

    def split(self, width):
        r, t, lat = self.rows, self.t, self.lat
        return [pl.BlockSpec((r.ctx_len, width), lambda b, s: (b, 0)),
                pl.BlockSpec((t, width), lambda b, s: (lat(b, s), 0))]

    def dec(self):
        r, t, off, lat = self.rows, self.t, self.flat_lat0, self.lat
        d = r.d
        return [pl.BlockSpec((r.ctx_len // CHUNK, DEC_ROWS, d), lambda b, s: (b, 0, 0)),
                pl.BlockSpec((t // CHUNK, DEC_ROWS, d), lambda b, s: (off + lat(b, s), 0, 0))]

    def out_shapes(self, dtype):
        r = self.rows
        return [jax.ShapeDtypeStruct((r.n_ctx, r.d), dtype), jax.ShapeDtypeStruct((r.n_lat, r.d), dtype)]


def _interleave(*pairs):
    return [p[0] for p in pairs] + [p[1] for p in pairs]


def _hgrn_scan_bwd(rows, packed, dec):
    d = rows.d
    blk = _ScanBlocks(rows, True)
    pk = lambda k: blk.flat(d, k)
    return pl.pallas_call(
        _scan_bwd_kernel,
        grid=(rows.bsz, blk.steps),
        in_specs=_interleave(pk(PK_QS + 2), pk(PK_KS + 2), pk(PK_V), blk.dec()),
        out_specs=blk.split(d),
        out_shape=blk.out_shapes(F32),
        scratch_shapes=[pltpu.VMEM((d // HEAD_DIM, HEAD_DIM, HEAD_DIM), F32)],
        compiler_params=_cparams(("arbitrary", "arbitrary")),
        name="hgrn_scan_bwd",
    )(*([packed, packed, packed, dec] * 2))


def _hgrn_scan_fwd(rows, packed, dec, o_bwd, xs, gnorm, mods, w_out, layer, j):
    d = rows.d
    blk = _ScanBlocks(rows, False)
    pk = lambda k: blk.flat(d, k)
    x_pair = blk.split(d) if len(xs) == 2 else blk.flat(d)
    x_args = list(xs) if len(xs) == 2 else [xs[0], xs[0]]
    mod_row = lambda b, s: jnp.where(s == 0, CTX_MOD_ROW, b)
    pairs = _interleave(pk(PK_QS), pk(PK_KS), pk(PK_V), blk.dec(), pk(PK_GATE), blk.split(d), x_pair)
    half = [packed, packed, packed, dec, packed]
    return pl.pallas_call(
        _scan_fwd_kernel,
        grid=(rows.bsz, blk.steps),
        in_specs=pairs + [_layer_vec_spec(d, j),
                          _mod_block(d, layer, 2, mod_row),
                          _const_spec((None, d, d), lambda b, s: (j, 0, 0))],
        out_specs=blk.split(d),
        out_shape=blk.out_shapes(F32),
        scratch_shapes=[pltpu.VMEM((d // HEAD_DIM, HEAD_DIM, HEAD_DIM), F32),
                        pltpu.VMEM((max(blk.t, rows.ctx_len), d), BF16)],
        compiler_params=_cparams(("arbitrary", "arbitrary")),
        name="hgrn_scan_fwd",
    )(*half, o_bwd[0], x_args[0], *half, o_bwd[1], x_args[1], gnorm, mods, w_out)


def _conv_taps(cw_ref, cb_ref, um, u, up):
    return cb_ref[...] + cw_ref[0:1, :] * um + cw_ref[1:2, :] * u + cw_ref[2:3, :] * up


def _conv_seq_kernel(n_ctx_tiles, ctx_len, x_ref, gain_ref, sh_ref, sc_ref, g1_ref, win_ref,
                     cw_ref, cb_ref, wout_ref, o_ref):
    x = x_ref[...]
    tm, d = x.shape
    h = _rms_mod(x, gain_ref[...], sh_ref[...], sc_ref[...]).astype(BF16)
    p = jnp.dot(h, win_ref[...], preferred_element_type=F32)
    gate_b, u = p[:, :d], p[:, d:2 * d] * p[:, 2 * d:]
    period = jnp.where(pl.program_id(0) < n_ctx_tiles, ctx_len, GRID_W)
    pos = lax.broadcasted_iota(jnp.int32, (tm, 1), 0) & (period - 1)
    um = jnp.where(pos == 0, 0.0, pltpu.roll(u, 1, axis=0))
    up = jnp.where(pos == period - 1, 0.0, pltpu.roll(u, tm - 1, axis=0))
    cv = _conv_taps(cw_ref, cb_ref, um, u, up)
    y = jnp.dot((gate_b * cv).astype(BF16), wout_ref[...], preferred_element_type=F32)
    o_ref[...] = x + g1_ref[...] * y


def _conv_rows_kernel(n_ctx_tiles, tiles_per_batch, xp_ref, x_ref, xn_ref, gain_ref, sh_ref,
                      sc_ref, g1_ref, win_ref, cw_ref, cb_ref, wout_ref, o_ref, h_ref):
    x = x_ref[...]
    tm, d = x.shape
    gain, sh, sc = gain_ref[...], sh_ref[...], sc_ref[...]
    h_ref[0:GRID_W, :] = _rms_mod(xp_ref[...], gain, sh, sc).astype(BF16)
    h_ref[GRID_W:GRID_W + tm, :] = _rms_mod(x, gain, sh, sc).astype(BF16)
    h_ref[GRID_W + tm:, :] = _rms_mod(xn_ref[...], gain, sh, sc).astype(BF16)
    p = jnp.dot(h_ref[...], win_ref[:, d:], preferred_element_type=F32)
    u = p[:, :d] * p[:, d:]
    gate_b = jnp.dot(h_ref[GRID_W:GRID_W + tm, :], win_ref[:, :d], preferred_element_type=F32)
    i = pl.program_id(0)
    jl = (i - n_ctx_tiles) % tiles_per_batch
    is_lat = i >= n_ctx_tiles
    has_prev = jnp.logical_and(is_lat, jl != 0)
    has_next = jnp.logical_and(is_lat, jl != tiles_per_batch - 1)
    row = lax.broadcasted_iota(jnp.int32, (tm, 1), 0)
    um = jnp.where(jnp.logical_or(row >= GRID_W, has_prev), u[0:tm], 0.0)
    up = jnp.where(jnp.logical_or(row < tm - GRID_W, has_next), u[2 * GRID_W:], 0.0)
    cv = _conv_taps(cw_ref, cb_ref, um, u[GRID_W:GRID_W + tm], up)
    y = jnp.dot((gate_b * cv).astype(BF16), wout_ref[...], preferred_element_type=F32)
    o_ref[...] = x + g1_ref[...] * y


def _conv_mixer(rows, x2, norm1, mods, w_in, cw, cb, w_out, layer, j, along_rows):
    d = rows.d
    tm = TM_CONV
    row = rows.mod_row(tm)
    n_tiles = rows.n_rows // tm
    n_ctx_tiles = rows.n_ctx // tm
    common = [
        _layer_vec_spec(d, layer),
        _mod_block(d, layer, 0, row),
        _mod_block(d, layer, 1, row),
        _mod_block(d, layer, 2, row),
        _const_spec((None, d, 3 * d), lambda i: (j, 0, 0)),
        pl.BlockSpec((None, 3, d), lambda i: (j, 0, 0)),
        _layer_vec_spec(d, j),
        _const_spec((None, d, d), lambda i: (j, 0, 0)),
    ]
    args = (norm1, mods, mods, mods, w_in, cw, cb, w_out)
    if along_rows:
        sub = tm // GRID_W
        last = rows.n_rows // GRID_W - 1
        kern = functools.partial(_conv_rows_kernel, n_ctx_tiles, rows.seq // tm)
        in_specs = [pl.BlockSpec((GRID_W, d), lambda i: (jnp.maximum(i * sub - 1, 0), 0)),
                    pl.BlockSpec((tm, d), lambda i: (i, 0)),
                    pl.BlockSpec((GRID_W, d), lambda i: (jnp.minimum((i + 1) * sub, last), 0))]
        args = (x2, x2, x2) + args
        scratch = [pltpu.VMEM((tm + 2 * GRID_W, d), BF16)]
        name = "conv_mixer_rows"
    else:
        kern = functools.partial(_conv_seq_kernel, n_ctx_tiles, rows.ctx_len)
        in_specs = [pl.BlockSpec((tm, d), lambda i: (i, 0))]
        args = (x2,) + args
        scratch = []
        name = "conv_mixer_seq"
    return pl.pallas_call(
        kern,
        grid=(n_tiles,),
        in_specs=in_specs + common,
        out_specs=pl.BlockSpec((tm, d), lambda i: (i, 0)),
        out_shape=jax.ShapeDtypeStruct((rows.n_rows, d), F32),
        scratch_shapes=scratch,
        compiler_params=_cparams(("parallel",)),
        name=name,
    )(*args)


def _mlp_residual(x, gain_ref, sh_ref, sc_ref, g2_ref, w1_ref, w2_ref):
    h = _rms_mod(x, gain_ref[...], sh_ref[...], sc_ref[...]).astype(BF16)
    d_ff = w1_ref.shape[-1]
    y = None
    for lo in range(0, d_ff, FF_CHUNK):
        a = jnp.dot(h, w1_ref[:, lo:lo + FF_CHUNK], preferred_element_type=F32)
        a = jnp.square(jnp.maximum(a, 0.0)).astype(BF16)
        part = jnp.dot(a, w2_ref[lo:lo + FF_CHUNK, :], preferred_element_type=F32)
        y = part if y is None else y + part
    return x + g2_ref[...] * y


def _mlp_kernel(n_src, ctx_tiles, *refs):
    x_refs, refs = _split_sources(n_src, refs)
    (gain_ref, sh_ref, sc_ref, g2_ref, w1_ref, w2_ref, nw1_ref, nw2_ref,
     o_ref, ow1_ref, ow2_ref) = refs
    x = _load_rows(x_refs, pl.program_id(0) < ctx_tiles)
    o_ref[...] = _mlp_residual(x, gain_ref, sh_ref, sc_ref, g2_ref, w1_ref, w2_ref)
    ow1_ref[...] = nw1_ref[...].astype(BF16)
    ow2_ref[...] = nw2_ref[...].astype(BF16)


def _mlp_final_kernel(x_ref, gain_ref, sh_ref, sc_ref, g2_ref, w1_ref, w2_ref, gf_ref, o_ref):
    x = _mlp_residual(x_ref[...], gain_ref, sh_ref, sc_ref, g2_ref, w1_ref, w2_ref)
    o_ref[...] = x * lax.rsqrt(jnp.mean(x * x, axis=-1, keepdims=True) + EPS) * gf_ref[...]


def _mlp(rows, xs, norm2, mods, w1, w2, layer, next_w=None, norm_f=None):
    d = rows.d
    d_ff = w1.shape[-1]
    tm = TM_MLP
    final = norm_f is not None
    assert final != (next_w is not None) and not (final and len(xs) != 1)
    first = rows.n_ctx // tm if final else 0
    n_out = rows.n_lat if final else rows.n_rows
    n_steps = n_out // tm
    row = lambda i: rows.mod_row(tm)(i + first)
    in_specs = _source_specs(rows, xs, tm, lambda i: i + first) + [
        _layer_vec_spec(d, layer),
        _mod_block(d, layer, 3, row),
        _mod_block(d, layer, 4, row),
        _mod_block(d, layer, 5, row),
        _const_spec((d, d_ff), lambda i: (0, 0)),
        _const_spec((d_ff, d), lambda i: (0, 0)),
    ]
    args = (*xs, norm2, mods, mods, mods, w1, w2)
    out_specs = [pl.BlockSpec((tm, d), lambda i: (i, 0))]
    out_shape = [jax.ShapeDtypeStruct((n_out, d), F32)]
    if final:
        in_specs.append(pl.BlockSpec((1, d), lambda i: (0, 0)))
        args += (norm_f.reshape(1, d),)
        kern = _mlp_final_kernel
    else:
        n_slabs = d_ff // MLP_W_SLAB
        assert n_slabs <= n_steps
        slab = lambda i: jnp.minimum(i, n_slabs - 1)
        in_specs += [pl.BlockSpec((None, d, MLP_W_SLAB), lambda i: (layer + 1, 0, slab(i))),
                     pl.BlockSpec((None, MLP_W_SLAB, d), lambda i: (layer + 1, slab(i), 0))]
        args += tuple(next_w)
        out_specs += [pl.BlockSpec((d, MLP_W_SLAB), lambda i: (0, slab(i))),
                      pl.BlockSpec((MLP_W_SLAB, d), lambda i: (slab(i), 0))]
        out_shape += [jax.ShapeDtypeStruct((d, d_ff), BF16), jax.ShapeDtypeStruct((d_ff, d), BF16)]
        kern = functools.partial(_mlp_kernel, len(xs), rows.n_ctx // tm)
    res = pl.pallas_call(
        kern,
        grid=(n_steps,),
        in_specs=in_specs,
        out_specs=out_specs,
        out_shape=out_shape,
        compiler_params=_cparams(("arbitrary",)),
        name="mlp_final" if final else "mlp",
    )(*args)
    return res[0] if final else res


def kernel(x, c, ctx, c_ctx, ada_w, ada_b, norm1, norm2, norm_f, mlp_w1, mlp_w2, hgrn_w_in, hgrn_lb,
           hgrn_gnorm, hgrn_w_out, conv_w_in, conv_w, conv_b, conv_w_out):
    bsz, seq, d = x.shape
    ctx_len = ctx.shape[1]
    depth = ada_w.shape[0]
    assert bsz <= CTX_MOD_ROW and d % HEAD_DIM == 0
    assert seq % max(TM_MLP, TM_CONV, T_SCAN) == 0 and ctx_len % CHUNK == 0
    assert (bsz * ctx_len) % max(TM_MLP, TM_CONV, T_SCAN) == 0
    assert ctx_len & (ctx_len - 1) == 0 and TM_CONV % ctx_len == 0
    rows = _Rows(bsz, seq, ctx_len, d)

    c8 = jnp.concatenate([c, c_ctx[None, :], jnp.zeros((MOD_ROWS - bsz - 1, d), F32)], axis=0)
    mods = _ada_table(c8, ada_w, ada_b).reshape(depth * MOD_ROWS * N_MODS, 1, d)
    lb = _lower_bounds(hgrn_lb).reshape(2, -1, 1, d)

    norm1 = norm1.reshape(depth, 1, d)
    norm2 = norm2.reshape(depth, 1, d)
    gnorm = hgrn_gnorm.reshape(-1, 1, d)
    conv_b = conv_b.reshape(-1, 1, d)
    w1, w2 = mlp_w1[0].astype(BF16), mlp_w2[0].astype(BF16)
    hw_in, hw_out = _prep_weight_pieces(hgrn_w_in), hgrn_w_out.astype(BF16)
    cw_in, cw_out = conv_w_in.astype(BF16), conv_w_out.astype(BF16)

    xs = (ctx.reshape(bsz * ctx_len, d), x.reshape(bsz * seq, d))
    for i in range(depth):
        j = i // N_MIXERS
        if i % N_MIXERS == 0:
            packed, dec = _hgrn_prep(rows, xs, norm1, mods, hw_in, lb, i, j)
            o_bwd = _hgrn_scan_bwd(rows, packed, dec)
            xs = tuple(_hgrn_scan_fwd(rows, packed, dec, o_bwd, xs, gnorm, mods, hw_out, i, j))
        else:
            xs = (_conv_mixer(rows, xs[0], norm1, mods, cw_in, conv_w, conv_b, cw_out, i, j,
                              along_rows=(j % 2 == 1)),)
        if i < depth - 1:
            x2, w1, w2 = _mlp(rows, xs, norm2, mods, w1, w2, i, next_w=(mlp_w1, mlp_w2))
        else:
            x2 = _mlp(rows, xs, norm2, mods, w1, w2, i, norm_f=norm_f)
        xs = (x2,)
    return xs[0].reshape(bsz, seq, d)
```
